```python
import jax
import jax.numpy as jnp
from jax import lax
import numpy as np

D_MODEL = 1024
BATCH = 8
SEQ = 4096
DEPTH = 4

GRID_W = 64
CTX_LEN = 256
EPS = 1e-6
N_BRANCH = 4
MIX_W = D_MODEL // 4

ML_HEADS = 4
ML_W = MIX_W
ML_DH = ML_W // ML_HEADS
ML_CHUNK = 64
ML_CONV = 3
ML_STATE_COLS = 3 * ML_W + 4 * ML_HEADS
ML_COLS = ML_STATE_COLS + ML_W
GM_GROUPS = 4
GM_CHUNK = 128
GM_W = MIX_W
GM_DG = GM_W // GM_GROUPS
CV_W = MIX_W
CV_K = 31
FT_GROUPS = 4
FT_W = MIX_W
FT_DG = FT_W // FT_GROUPS
GM_OFF = ML_COLS
CV_OFF = GM_OFF + 2 * GM_W
FT_OFF = CV_OFF + 2 * CV_W
GATE_OFF = FT_OFF + FT_W
IN_COLS = GATE_OFF + N_BRANCH * D_MODEL
PEER_HEADS = 8
PEER_TOPK = 16
N_KEYS = 128
N_EXPERTS = N_KEYS * N_KEYS
PEER_DQ = 256
PEER_DK_HALF = PEER_DQ // 2
PEER_BLOCK = 128

kernel_name = 'hybrid_flow_mlstm_gmlp_conformer_fnet_peer'


def _rmsnorm(x, g):
    xf = x.astype(jnp.float32)
    y = xf * lax.rsqrt(jnp.mean(xf * xf, axis=-1, keepdims=True) + EPS)
    return (y * g.astype(jnp.float32)).astype(x.dtype)


def _layernorm(x, g, b):
    xf = x.astype(jnp.float32)
    mu = jnp.mean(xf, axis=-1, keepdims=True)
    xc = xf - mu
    y = xc * lax.rsqrt(jnp.mean(xc * xc, axis=-1, keepdims=True) + EPS)
    return (y * g.astype(jnp.float32) + b.astype(jnp.float32)).astype(x.dtype)


def _modulate(h, shift, scale):
    return h * (1 + scale) + shift


def _dwconv(x, w, b):
    k, ch = w.shape
    y = lax.conv_general_dilated(x, w[:, None, :].astype(x.dtype), window_strides=(1,),
                                 padding=((k // 2, k // 2),),
                                 dimension_numbers=('NWC', 'WIO', 'NWC'),
                                 feature_group_count=ch)
    return y + b


def _grid_sincos(n_tok, d):
    rows = n_tok // GRID_W
    n_freq = d // 4
    freq = 1.0 / (10000.0 ** (jnp.arange(n_freq, dtype=jnp.float32) / n_freq))
    r = jnp.repeat(jnp.arange(rows, dtype=jnp.float32), GRID_W)
    cc = jnp.tile(jnp.arange(GRID_W, dtype=jnp.float32), rows)
    ar = r[:, None] * freq[None, :]
    ac = cc[:, None] * freq[None, :]
    return jnp.concatenate([jnp.sin(ar), jnp.cos(ar), jnp.sin(ac), jnp.cos(ac)], axis=-1)


def _zero_state(b_sz):
    g = 2 * ML_HEADS
    return (jnp.zeros((b_sz, g, ML_DH, ML_DH), jnp.float32),
            jnp.zeros((b_sz, g, ML_DH), jnp.float32),
            jnp.zeros((b_sz, g), jnp.float32))


def _mlstm_scan(q, k, v, li, lf, state, need_out):
    b_sz, g, t_len, dh = q.shape
    nc = t_len // ML_CHUNK

    def chunks(a):
        a = a.reshape((b_sz, g, nc, ML_CHUNK) + a.shape[3:])
        return jnp.moveaxis(a, 2, 0)

    tril = jnp.tril(jnp.ones((ML_CHUNK, ML_CHUNK), dtype=bool))

    def step(carry, inp):
        c_st, n_st, m_st = carry
        qc, kc, vc, lic, lfc = inp
        bcum = jnp.cumsum(lfc, axis=-1)
        b_last = bcum[..., -1]
        w = b_last[..., None] - bcum + lic
        m_new = jnp.maximum(b_last + m_st, jnp.max(w, axis=-1))
        ws = jnp.exp(w - m_new[..., None])
        decay = jnp.exp(b_last + m_st - m_new)
        c_new = decay[..., None, None] * c_st + jnp.einsum('bgs,bgsd,bgse->bgde', ws, kc, vc)
        n_new = decay[..., None] * n_st + jnp.einsum('bgs,bgsd->bgd', ws, kc)
        if not need_out:
            return (c_new, n_new, m_new), None
        dlog = jnp.where(tril, bcum[..., :, None] - bcum[..., None, :] + lic[..., None, :], -jnp.inf)
        inter = bcum + m_st[..., None]
        m_t = jnp.maximum(inter, jnp.max(dlog, axis=-1))
        s = jnp.einsum('bgtd,bgsd->bgts', qc, kc) * jnp.exp(dlog - m_t[..., None])
        gi = jnp.exp(inter - m_t)
        num = gi[..., None] * jnp.einsum('bgtd,bgde->bgte', qc, c_st) + jnp.einsum('bgts,bgse->bgte', s, vc)
        den = gi * jnp.einsum('bgtd,bgd->bgt', qc, n_st) + jnp.sum(s, axis=-1)
        h = num / jnp.maximum(jnp.abs(den), jnp.exp(-m_t))[..., None]
        return (c_new, n_new, m_new), h

    state, h = lax.scan(step, state, (chunks(q), chunks(k), chunks(v), chunks(li), chunks(lf)))
    if need_out:
        h = jnp.moveaxis(h, 0, 2).reshape(b_sz, g, t_len, dh)
    return h, state


def _mlstm_branch(p, conv_w, conv_b, gate_b, norm_g, state, need_out):
    b_sz, t_len, _ = p.shape
    qk = jax.nn.silu(_dwconv(p[..., :2 * ML_W], conv_w, conv_b))

    def heads(a):
        return a.astype(jnp.float32).reshape(b_sz, t_len, ML_HEADS, ML_DH).transpose(0, 2, 1, 3)

    q = heads(qk[..., :ML_W])
    k = heads(qk[..., ML_W:]) * (ML_DH ** -0.5)
    v = heads(p[..., 2 * ML_W:3 * ML_W])
    gates = (p[..., 3 * ML_W:ML_STATE_COLS] + gate_b).astype(jnp.float32).transpose(0, 2, 1)
    li = gates[:, :2 * ML_HEADS]
    lf = jax.nn.log_sigmoid(gates[:, 2 * ML_HEADS:])

    def both(a):
        return jnp.concatenate([a, jnp.flip(a, axis=2)], axis=1)

    def dirs(a):
        return jnp.concatenate([a[:, :ML_HEADS], jnp.flip(a[:, ML_HEADS:], axis=2)], axis=1)

    h2, state = _mlstm_scan(both(q), both(k), both(v), dirs(li), dirs(lf), state, need_out)
    if not need_out:
        return None, state
    h = h2[:, :ML_HEADS] + jnp.flip(h2[:, ML_HEADS:], axis=2)
    h = h * lax.rsqrt(jnp.mean(h * h, axis=-1, keepdims=True) + EPS) * norm_g.astype(jnp.float32)[None, :, None, :]
    h = h.transpose(0, 2, 1, 3).reshape(b_sz, t_len, ML_W)
    o = jax.nn.sigmoid(p[..., ML_STATE_COLS:ML_COLS].astype(jnp.float32))
    return (h * o).astype(p.dtype), state


def _gmlp_branch(p, ln_g, ln_b, w_s, b_s):
    b_sz, t_len, _ = p.shape
    z = jax.nn.gelu(p)
    u, v = z[..., :GM_W], z[..., GM_W:]
    v = _layernorm(v, ln_g, ln_b).reshape(b_sz, t_len // GM_CHUNK, GM_CHUNK, GM_GROUPS, GM_DG)
    s = jnp.einsum('gts,bnsgc->bntgc', w_s, v) + b_s.T[:, :, None]
    return u * s.reshape(b_sz, t_len, GM_W)


def _conv_branch(p, dw_w, dw_b, ln_g, ln_b):
    z = p[..., :CV_W] * jax.nn.sigmoid(p[..., CV_W:])
    z = _dwconv(z, dw_w, dw_b)
    return jax.nn.silu(_layernorm(z, ln_g, ln_b))


def _fourier_branch(p):
    b_sz, t_len, _ = p.shape
    z = p.astype(jnp.float32).reshape(b_sz, t_len, FT_GROUPS, FT_DG)
    z = jnp.fft.fft2(z, axes=(1, 3), norm='ortho').real
    return z.reshape(b_sz, t_len, FT_W).astype(p.dtype)


def _mixer_merge(p, ml_out, gm_ln_g, gm_ln_b, gm_w_s, gm_b_s, cv_dw_w, cv_dw_b, cv_ln_g, cv_ln_b, w_branch, w_out):
    branches = (ml_out,
                _gmlp_branch(p[..., GM_OFF:CV_OFF], gm_ln_g, gm_ln_b, gm_w_s, gm_b_s),
                _conv_branch(p[..., CV_OFF:FT_OFF], cv_dw_w, cv_dw_b, cv_ln_g, cv_ln_b),
                _fourier_branch(p[..., FT_OFF:GATE_OFF]))
    y = None
    for i, z in enumerate(branches):
        gate = jax.nn.sigmoid(p[..., GATE_OFF + i * D_MODEL:GATE_OFF + (i + 1) * D_MODEL])
        term = gate * (z @ w_branch[i])
        y = term if y is None else y + term
    return y @ w_out


def _peer(h, w_q, keys, u_tab, v_tab):
    b_sz, t_len, d = h.shape
    hb_all = h.reshape(-1, PEER_BLOCK, d)

    def block(hb):
        n = hb.shape[0]
        q = (hb @ w_q).reshape(n, PEER_HEADS, 2, PEER_DK_HALF)
        s = jnp.einsum('nhpk,pek->nhpe', q, keys)
        v_top, i_top = lax.top_k(s, PEER_TOPK)
        cand = (v_top[:, :, 0, :, None] + v_top[:, :, 1, None, :]).reshape(n, PEER_HEADS, PEER_TOPK * PEER_TOPK)
        sc, pos = lax.top_k(cand, PEER_TOPK)
        i1 = jnp.take_along_axis(i_top[:, :, 0], pos // PEER_TOPK, axis=-1)
        i2 = jnp.take_along_axis(i_top[:, :, 1], pos % PEER_TOPK, axis=-1)
        e = i1 * N_KEYS + i2
        g = jax.nn.softmax(sc.astype(jnp.float32), axis=-1).astype(hb.dtype)
        a = jax.nn.gelu(jnp.einsum('nhkd,nd->nhk', u_tab[e], hb)) * g
        return jnp.einsum('nhk,nhkd->nd', a, v_tab[e])

    return lax.map(block, hb_all).reshape(b_sz, t_len, d)


def setup_inputs(seed: int = 0) -> dict:
    key = jax.random.key(seed)
    ks = jax.random.split(key, 32)
    f32 = jnp.float32
    L, D = DEPTH, D_MODEL

    def nrm(k, shape, s):
        return jax.random.normal(k, shape, f32) * s

    def gain(k, shape):
        return 1.0 + nrm(k, shape, 0.02)

    ml_gate_b = jnp.concatenate([nrm(ks[11], (L, 2 * ML_HEADS), 0.1),
                                 jax.random.uniform(ks[12], (L, 2 * ML_HEADS), f32, 3.0, 6.0)], axis=-1)
    return {
        'x': nrm(ks[0], (BATCH, SEQ, D), 1.0),
        'c': nrm(ks[1], (BATCH, D), 1.0),
        'ctx': nrm(ks[2], (BATCH, CTX_LEN, D), 1.0),
        'c_ctx': nrm(ks[3], (D,), 1.0),
        'w_ada': nrm(ks[4], (L, D, 6 * D), 0.5 * D ** -0.5),
        'b_ada': nrm(ks[5], (L, 6 * D), 0.02),
        'norm1_g': gain(ks[6], (L, D)),
        'norm2_g': gain(ks[7], (L, D)),
        'w_in': nrm(ks[8], (L, D, IN_COLS), D ** -0.5),
        'ml_conv_w': nrm(ks[9], (L, ML_CONV, 2 * ML_W), ML_CONV ** -0.5),
        'ml_conv_b': nrm(ks[10], (L, 2 * ML_W), 0.02),
        'ml_gate_b': ml_gate_b,
        'ml_norm_g': gain(ks[13], (L, ML_HEADS, ML_DH)),
        'gm_ln_g': gain(ks[14], (L, GM_W)),
        'gm_ln_b': nrm(ks[15], (L, GM_W), 0.02),
        'gm_w_s': nrm(ks[16], (L, GM_GROUPS, GM_CHUNK, GM_CHUNK), GM_CHUNK ** -0.5),
        'gm_b_s': gain(ks[17], (L, GM_GROUPS, GM_CHUNK)),
        'cv_dw_w': nrm(ks[18], (L, CV_K, CV_W), CV_K ** -0.5),
        'cv_dw_b': nrm(ks[19], (L, CV_W), 0.02),
        'cv_ln_g': gain(ks[20], (L, CV_W)),
        'cv_ln_b': nrm(ks[21], (L, CV_W), 0.02),
        'w_branch': nrm(ks[22], (L, N_BRANCH, MIX_W, D), MIX_W ** -0.5),
        'w_out': nrm(ks[23], (L, D, D), D ** -0.5),
        'peer_w_q': nrm(ks[24], (L, D, PEER_HEADS * PEER_DQ), D ** -0.5),
        'peer_keys': nrm(ks[25], (L, 2, N_KEYS, PEER_DK_HALF), PEER_DK_HALF ** -0.5),
        'peer_u': nrm(ks[26], (L, N_EXPERTS, D), D ** -0.5),
        'peer_v': nrm(ks[27], (L, N_EXPERTS, D), PEER_HEADS ** -0.5),
        'final_norm_g': gain(ks[28], (D,)),
    }


def reference(x, c, ctx, c_ctx, w_ada, b_ada, norm1_g, norm2_g, w_in, ml_conv_w, ml_conv_b, ml_gate_b,
              ml_norm_g, gm_ln_g, gm_ln_b, gm_w_s, gm_b_s, cv_dw_w, cv_dw_b, cv_ln_g, cv_ln_b, w_branch,
              w_out, peer_w_q, peer_keys, peer_u, peer_v, final_norm_g):
    b_sz, n_tok, d = x.shape
    x = x + _grid_sincos(n_tok, d).astype(x.dtype)[None]
    xc = ctx
    s_lat = jax.nn.silu(c)[:, None, :]
    s_ctx = jax.nn.silu(c_ctx)
    for l in range(DEPTH):
        last = l == DEPTH - 1
        mod = jnp.split(s_lat @ w_ada[l] + b_ada[l], 6, axis=-1)
        modc = jnp.split(s_ctx @ w_ada[l] + b_ada[l], 6, axis=-1)
        h = _modulate(_rmsnorm(x, norm1_g[l]), mod[0], mod[1])
        hc = _modulate(_rmsnorm(xc, norm1_g[l]), modc[0], modc[1])
        p = h @ w_in[l]
        pc = hc @ (w_in[l][:, :ML_STATE_COLS] if last else w_in[l])
        ml_args = (ml_conv_w[l], ml_conv_b[l], ml_gate_b[l], ml_norm_g[l])
        ml_c, state = _mlstm_branch(pc, *ml_args, _zero_state(b_sz), not last)
        ml_x, _ = _mlstm_branch(p, *ml_args, state, True)
        mix_args = (gm_ln_g[l], gm_ln_b[l], gm_w_s[l], gm_b_s[l], cv_dw_w[l], cv_dw_b[l],
                    cv_ln_g[l], cv_ln_b[l], w_branch[l], w_out[l])
        x = x + mod[2] * _mixer_merge(p, ml_x, *mix_args)
        peer_args = (peer_w_q[l], peer_keys[l], peer_u[l], peer_v[l])
        h = _modulate(_rmsnorm(x, norm2_g[l]), mod[3], mod[4])
        x = x + mod[5] * _peer(h, *peer_args)
        if not last:
            xc = xc + modc[2] * _mixer_merge(pc, ml_c, *mix_args)
            hc = _modulate(_rmsnorm(xc, norm2_g[l]), modc[3], modc[4])
            xc = xc + modc[5] * _peer(hc, *peer_args)
    return _rmsnorm(x, final_norm_g)
```

```python
import functools
import math

import jax
import jax.numpy as jnp
from jax import lax
from jax.experimental import pallas as pl
from jax.experimental.pallas import tpu as pltpu

F32 = jnp.float32
BF16 = jnp.bfloat16
HIGHEST = lax.Precision.HIGHEST

D_MODEL = 1024
GRID_W = 64
EPS = 1e-6
MIX_W = 256
ML_HEADS = 4
ML_DH = 64
ML_CHUNK = 128
GM_CHUNK = 128
GM_GROUPS = 4
GM_DG = 64
CV_K = 31
ML_CONV = 3
FT_GROUPS = 4
FT_DG = 64
PEER_HEADS = 8
PEER_TOPK = 16
N_KEYS = 128
PEER_SEL = PEER_HEADS * PEER_TOPK

LANE = 128
HEAD_PAD = LANE
ML_PAD_W = ML_HEADS * HEAD_PAD
CONV_HALO = 16
VMEM_LIMIT = 56 * 1024 * 1024


def _cparams(*sem):
    return pltpu.CompilerParams(dimension_semantics=sem, vmem_limit_bytes=VMEM_LIMIT)


def _sigmoid(x):
    return 1.0 / (1.0 + jnp.exp(-x))


def _silu(x):
    return x * _sigmoid(x)


def _gelu(x):
    return 0.5 * x * (1.0 + jnp.tanh(math.sqrt(2.0 / math.pi) * (x + 0.044715 * (x * x * x))))


def _log_sigmoid(x):
    return jnp.minimum(x, 0.0) - jnp.log(1.0 + jnp.exp(-jnp.abs(x)))


def _rms_mod(x, g, shift, scale):
    y = x * lax.rsqrt(jnp.mean(x * x, axis=-1, keepdims=True) + EPS) * g
    return y * (1.0 + scale) + shift


def _layernorm(x, g, b):
    mu = jnp.mean(x, axis=-1, keepdims=True)
    xc = x - mu
    return xc * lax.rsqrt(jnp.mean(xc * xc, axis=-1, keepdims=True) + EPS) * g + b


def _dot(a, b):
    return jnp.dot(a, b, preferred_element_type=F32)


def _dot_nt(a, b):
    return lax.dot_general(a, b, (((1,), (1,)), ((), ())), preferred_element_type=F32)


def _dot_exact(a, b):
    return jnp.dot(a, b, preferred_element_type=F32, precision=HIGHEST)


def _ada_kernel(s_ref, w_ref, b_ref, o_ref):
    s = _silu(s_ref[...])
    o_ref[0] = _dot_exact(s, w_ref[0]) + b_ref[0]


def _ada_call(cond, w_ada, b_ada):
    depth, d, n6 = w_ada.shape
    rows = cond.shape[0]
    tn = 1536
    return pl.pallas_call(
        _ada_kernel,
        out_shape=jax.ShapeDtypeStruct((depth, rows, n6), F32),
        grid=(depth, n6 // tn),
        in_specs=[pl.BlockSpec((rows, d), lambda l, j: (0, 0)),
                  pl.BlockSpec((1, d, tn), lambda l, j: (l, 0, j)),
                  pl.BlockSpec((1, 1, tn), lambda l, j: (l, 0, j))],
        out_specs=pl.BlockSpec((1, rows, tn), lambda l, j: (l, 0, j)),
        compiler_params=_cparams("arbitrary", "arbitrary"),
        name="ada_mod",
    )(cond, w_ada, b_ada.reshape(depth, 1, n6))


def _addpos_kernel(x_ref, p_ref, o_ref):
    o_ref[0] = x_ref[0] + p_ref[...]


def _addpos_call(x, pos):
    b, t, d = x.shape
    tm = min(t, 512)
    return pl.pallas_call(
        _addpos_kernel,
        out_shape=jax.ShapeDtypeStruct(x.shape, F32),
        grid=(t // tm, b),
        in_specs=[pl.BlockSpec((1, tm, d), lambda i, j: (j, i, 0)),
                  pl.BlockSpec((tm, d), lambda i, j: (i, 0))],
        out_specs=pl.BlockSpec((1, tm, d), lambda i, j: (j, i, 0)),
        compiler_params=_cparams("arbitrary", "arbitrary"),
        name="add_pos",
    )(x, pos)


def _mod_spec(row_fn, k):
    return pl.BlockSpec((1, 1, 1, D_MODEL), lambda *g: (row_fn(*g), k, 0, 0))


def _const_spec(shape):
    nd = len(shape)
    return pl.BlockSpec(shape, lambda *g: (0,) * nd)


def _inproj_ml_kernel(x_ref, sh_ref, sc_ref, g_ref, w_ref, wgc_ref, wgr_ref, p_ref, gc_ref, gr_ref):
    h = _rms_mod(x_ref[0], g_ref[...], sh_ref[0, 0], sc_ref[0, 0]).astype(BF16)
    p_ref[0] = _dot(h, w_ref[...])
    gc_ref[0] = _dot(h, wgc_ref[...])
    gr_ref[0] = _dot_nt(wgr_ref[...], h)


def _inproj_ml_call(x, mod, row_fn, norm_g, w_ml, w_gc, w_gr):
    b, t, d = x.shape
    tm = 256
    n_ml = w_ml.shape[1]
    return pl.pallas_call(
        _inproj_ml_kernel,
        out_shape=(jax.ShapeDtypeStruct((b, t, n_ml), F32),
                   jax.ShapeDtypeStruct((b, t, LANE), F32),
                   jax.ShapeDtypeStruct((b, 16, t), F32)),
        grid=(b, t // tm),
        in_specs=[pl.BlockSpec((1, tm, d), lambda i, j: (i, j, 0)),
                  _mod_spec(lambda i, j: row_fn(i), 0),
                  _mod_spec(lambda i, j: row_fn(i), 1),
                  _const_spec((1, d)),
                  _const_spec(w_ml.shape),
                  _const_spec(w_gc.shape),
                  _const_spec(w_gr.shape)],
        out_specs=(pl.BlockSpec((1, tm, n_ml), lambda i, j: (i, j, 0)),
                   pl.BlockSpec((1, tm, LANE), lambda i, j: (i, j, 0)),
                   pl.BlockSpec((1, 16, tm), lambda i, j: (i, 0, j))),
        compiler_params=_cparams("arbitrary", "arbitrary"),
        name="inproj_ml",
    )(x, mod, mod, norm_g, w_ml, w_gc, w_gr)


def _inproj_rest_kernel(x_ref, sh_ref, sc_ref, g_ref, w_ref, lng_ref, lnb_ref, wcat_ref, bgm_ref, cs_ref,
                        zgm_ref, zglu_ref, fa_ref, fb_ref, gate_ref):
    tm = x_ref.shape[1]
    h = _rms_mod(x_ref[0], g_ref[...], sh_ref[0, 0], sc_ref[0, 0]).astype(BF16)
    z = _gelu(_dot(h, w_ref[:, 0:2 * MIX_W]))
    u = z[:, :MIX_W]
    v = _layernorm(z[:, MIX_W:], lng_ref[...], lnb_ref[...])
    grp = lax.broadcasted_iota(jnp.int32, (GM_CHUNK, MIX_W), 1) // GM_DG
    for j in range(tm // GM_CHUNK):
        vc = v[j * GM_CHUNK:(j + 1) * GM_CHUNK]
        vstack = jnp.concatenate([jnp.where(grp == g, vc, 0.0) for g in range(GM_GROUPS)], axis=0).astype(BF16)
        s = _dot(wcat_ref[...], vstack) + bgm_ref[...]
        zgm_ref[0, j * GM_CHUNK:(j + 1) * GM_CHUNK, :] = u[j * GM_CHUNK:(j + 1) * GM_CHUNK] * s
    pc = _dot(h, w_ref[:, 2 * MIX_W:4 * MIX_W])
    zglu_ref[0] = pc[:, :MIX_W] * _sigmoid(pc[:, MIX_W:])
    pf = _dot(h, w_ref[:, 4 * MIX_W:5 * MIX_W]).astype(BF16)
    ab = _dot(pf, cs_ref[...])
    fa_ref[...] = ab[:, :MIX_W].astype(BF16)
    fb_ref[...] = ab[:, MIX_W:].astype(BF16)
    for i in range(4):
        lo = 5 * MIX_W + i * D_MODEL
        gate_ref[0, :, i * D_MODEL:(i + 1) * D_MODEL] = _sigmoid(_dot(h, w_ref[:, lo:lo + D_MODEL])).astype(BF16)


def _inproj_rest_call(x, mod, row_fn, norm_g, w_rest, gm_ln_g, gm_ln_b, wcat, bias_gm, cs):
    b, t, d = x.shape
    tm = 256
    return pl.pallas_call(
        _inproj_rest_kernel,
        out_shape=(jax.ShapeDtypeStruct((b, t, MIX_W), F32),
                   jax.ShapeDtypeStruct((b, t, MIX_W), F32),
                   jax.ShapeDtypeStruct((t, b * MIX_W), BF16),
                   jax.ShapeDtypeStruct((t, b * MIX_W), BF16),
                   jax.ShapeDtypeStruct((b, t, 4 * D_MODEL), BF16)),
        grid=(b, t // tm),
        in_specs=[pl.BlockSpec((1, tm, d), lambda i, j: (i, j, 0)),
                  _mod_spec(lambda i, j: row_fn(i), 0),
                  _mod_spec(lambda i, j: row_fn(i), 1),
                  _const_spec((1, d)),
                  _const_spec(w_rest.shape),
                  _const_spec((1, MIX_W)), _const_spec((1, MIX_W)),
                  _const_spec(wcat.shape), _const_spec(bias_gm.shape), _const_spec(cs.shape)],
        out_specs=(pl.BlockSpec((1, tm, MIX_W), lambda i, j: (i, j, 0)),
                   pl.BlockSpec((1, tm, MIX_W), lambda i, j: (i, j, 0)),
                   pl.BlockSpec((tm, MIX_W), lambda i, j: (j, i)),
                   pl.BlockSpec((tm, MIX_W), lambda i, j: (j, i)),
                   pl.BlockSpec((1, tm, 4 * D_MODEL), lambda i, j: (i, j, 0))),
        compiler_params=_cparams("arbitrary", "arbitrary"),
        name="inproj_rest",
    )(x, mod, mod, norm_g, w_rest, gm_ln_g, gm_ln_b, wcat, bias_gm, cs)


def _dwconv_kernel(x_ref, prev_ref, next_ref, w_ref, b_ref, o_ref, win_ref, *, taps, act):
    i = pl.program_id(2)
    last = pl.num_programs(2) - 1
    rc = x_ref.shape[1]
    win_ref[0:CONV_HALO, :] = jnp.where(i > 0, prev_ref[0], 0.0)
    win_ref[CONV_HALO:CONV_HALO + rc, :] = x_ref[0]
    win_ref[CONV_HALO + rc:, :] = jnp.where(i < last, next_ref[0], 0.0)
    sub = 128
    for r in range(rc // sub):
        acc = jnp.broadcast_to(b_ref[...], (sub, LANE))
        for j in range(taps):
            lo = CONV_HALO + r * sub + j - taps // 2
            acc = acc + w_ref[j:j + 1, :] * win_ref[lo:lo + sub, :]
        if act:
            acc = _silu(acc)
        o_ref[0, r * sub:(r + 1) * sub, :] = acc


def _dwconv_call(x, col0, ncol, w, bias, act):
    b, t, _ = x.shape
    taps = w.shape[0]
    rc = min(t, 512)
    c0 = col0 // LANE
    hb = rc // CONV_HALO
    nh = t // CONV_HALO
    return pl.pallas_call(
        functools.partial(_dwconv_kernel, taps=taps, act=act),
        out_shape=jax.ShapeDtypeStruct((b, t, ncol), F32),
        grid=(b, ncol // LANE, t // rc),
        in_specs=[pl.BlockSpec((1, rc, LANE), lambda bi, c, i: (bi, i, c0 + c)),
                  pl.BlockSpec((1, CONV_HALO, LANE), lambda bi, c, i: (bi, jnp.maximum(i * hb - 1, 0), c0 + c)),
                  pl.BlockSpec((1, CONV_HALO, LANE), lambda bi, c, i: (bi, jnp.minimum((i + 1) * hb, nh - 1), c0 + c)),
                  pl.BlockSpec((taps, LANE), lambda bi, c, i: (0, c)),
                  pl.BlockSpec((1, LANE), lambda bi, c, i: (0, c))],
        out_specs=pl.BlockSpec((1, rc, LANE), lambda bi, c, i: (bi, i, c)),
        scratch_shapes=[pltpu.VMEM((rc + 2 * CONV_HALO, LANE), F32)],
        compiler_params=_cparams("arbitrary", "arbitrary", "arbitrary"),
        name="dwconv_k%d" % taps,
    )(x, x, x, w, bias)


def _mlstm_kernel(*refs, need_out):
    (qkf_ref, qkb_ref, vf_ref, vb_ref, gcf_ref, gcb_ref, grf_ref, grb_ref, gbc_ref, gbr_ref,
     cn_in_ref, m_in_ref) = refs[:12]
    if need_out:
        hf_ref, hb_ref, cn_out_ref, m_out_ref, cn_s, m_s = refs[12:]
    else:
        cn_out_ref, m_out_ref, cn_s, m_s = refs[12:]
        hf_ref = hb_ref = None
    j = pl.program_id(1)
    last = pl.num_programs(1) - 1
    L = ML_CHUNK

    @pl.when(j == 0)
    def _():
        cn_s[...] = cn_in_ref[0]
        m_s[...] = m_in_ref[0]

    row = lax.broadcasted_iota(jnp.int32, (L, L), 0)
    col = lax.broadcasted_iota(jnp.int32, (L, L), 1)
    lane = lax.broadcasted_iota(jnp.int32, (1, HEAD_PAD), 1)
    one_at_dh = (lane == ML_DH).astype(F32)
    keep_dh = (lane < ML_DH).astype(F32)

    for direction in range(2):
        qk_ref, v_ref, gc_ref, gr_ref, h_ref = ((qkf_ref, vf_ref, gcf_ref, grf_ref, hf_ref) if direction == 0
                                                else (qkb_ref, vb_ref, gcb_ref, grb_ref, hb_ref))
        visible = (col <= row) if direction == 0 else (col >= row)
        tri_c = visible.astype(F32)
        tri_r = ((row <= col) if direction == 0 else (row >= col)).astype(F32)
        gcv = gc_ref[0] + gbc_ref[...]
        grv = gr_ref[0] + gbr_ref[...]
        lf_c = _log_sigmoid(gcv)
        lf_r = _log_sigmoid(grv)
        cum_c = _dot_exact(tri_c, lf_c)
        cum_r = _dot_exact(lf_r, tri_r)
        tot_c = jnp.sum(lf_c, axis=0, keepdims=True)
        for hd in range(ML_HEADS):
            g = direction * ML_HEADS + hd
            gi_col = g
            gf_col = 2 * ML_HEADS + g
            li_r = grv[gi_col:gi_col + 1, :]
            li_c = gcv[:, gi_col:gi_col + 1]
            cc = cum_c[:, gf_col:gf_col + 1]
            cr = cum_r[gf_col:gf_col + 1, :]
            tot = tot_c[:, gf_col:gf_col + 1]
            m_prev = m_s[g][:, 0:1]
            cn = cn_s[g]
            qh = qk_ref[0, :, hd * HEAD_PAD:(hd + 1) * HEAD_PAD]
            kh = qk_ref[0, :, ML_PAD_W + hd * HEAD_PAD:ML_PAD_W + (hd + 1) * HEAD_PAD] * (ML_DH ** -0.5)
            v1 = v_ref[0, :, hd * HEAD_PAD:(hd + 1) * HEAD_PAD] + one_at_dh
            w_r = tot - cr + li_r
            w_c = tot - cc + li_c
            m_new = jnp.maximum(tot + m_prev, jnp.max(w_r, axis=1, keepdims=True))
            ws_c = jnp.exp(w_c - m_new)
            decay = jnp.exp(tot + m_prev - m_new)
            if need_out:
                dlog = jnp.where(visible, cc - cr + li_r, -jnp.inf)
                inter = cc + m_prev
                m_t = jnp.maximum(inter, jnp.max(dlog, axis=1, keepdims=True))
                s = _dot_nt(qh.astype(BF16), kh.astype(BF16)) * jnp.exp(dlog - m_t)
                gi = jnp.exp(inter - m_t)
                nd = gi * _dot(qh.astype(BF16), cn.astype(BF16)) + _dot(s.astype(BF16), v1.astype(BF16))
                den = nd[:, ML_DH:ML_DH + 1]
                hh = nd / jnp.maximum(jnp.abs(den), jnp.exp(-m_t))
                h_ref[0, :, hd * HEAD_PAD:(hd + 1) * HEAD_PAD] = hh * keep_dh
            kw = (kh * ws_c).astype(BF16)
            upd = lax.dot_general(kw, v1.astype(BF16), (((0,), (0,)), ((), ())), preferred_element_type=F32)
            cn_s[g] = decay * cn + upd
            m_s[g] = jnp.broadcast_to(m_new, (1, LANE))

    @pl.when(j == last)
    def _():
        cn_out_ref[0] = cn_s[...]
        m_out_ref[0] = m_s[...]


def _mlstm_call(qk, p_ml, gc, gr, gbc, gbr, cn_in, m_in, need_out):
    b, t, _ = qk.shape
    L = ML_CHUNK
    nc = t // L
    g2 = 2 * ML_HEADS
    fwd = lambda i, j: (i, j, 0)
    bwd = lambda i, j: (i, nc - 1 - j, 0)
    in_specs = [pl.BlockSpec((1, L, 2 * ML_PAD_W), fwd),
                pl.BlockSpec((1, L, 2 * ML_PAD_W), bwd),
                pl.BlockSpec((1, L, ML_PAD_W), lambda i, j: (i, j, 2)),
                pl.BlockSpec((1, L, ML_PAD_W), lambda i, j: (i, nc - 1 - j, 2)),
                pl.BlockSpec((1, L, LANE), fwd),
                pl.BlockSpec((1, L, LANE), bwd),
                pl.BlockSpec((1, 16, L), lambda i, j: (i, 0, j)),
                pl.BlockSpec((1, 16, L), lambda i, j: (i, 0, nc - 1 - j)),
                _const_spec((1, LANE)),
                _const_spec((16, L)),
                pl.BlockSpec((1, g2, HEAD_PAD, HEAD_PAD), lambda i, j: (i, 0, 0, 0)),
                pl.BlockSpec((1, g2, 1, LANE), lambda i, j: (i, 0, 0, 0))]
    state_shapes = (jax.ShapeDtypeStruct((b, g2, HEAD_PAD, HEAD_PAD), F32),
                    jax.ShapeDtypeStruct((b, g2, 1, LANE), F32))
    state_specs = (pl.BlockSpec((1, g2, HEAD_PAD, HEAD_PAD), lambda i, j: (i, 0, 0, 0)),
                   pl.BlockSpec((1, g2, 1, LANE), lambda i, j: (i, 0, 0, 0)))
    if need_out:
        out_shape = (jax.ShapeDtypeStruct((b, t, ML_PAD_W), F32),
                     jax.ShapeDtypeStruct((b, t, ML_PAD_W), F32)) + state_shapes
        out_specs = (pl.BlockSpec((1, L, ML_PAD_W), fwd), pl.BlockSpec((1, L, ML_PAD_W), bwd)) + state_specs
    else:
        out_shape, out_specs = state_shapes, state_specs
    return pl.pallas_call(
        functools.partial(_mlstm_kernel, need_out=need_out),
        out_shape=out_shape,
        grid=(b, nc),
        in_specs=in_specs,
        out_specs=out_specs,
        scratch_shapes=[pltpu.VMEM((g2, HEAD_PAD, HEAD_PAD), F32), pltpu.VMEM((g2, 1, LANE), F32)],
        compiler_params=_cparams("arbitrary", "arbitrary"),
        name="mlstm_out" if need_out else "mlstm_state",
    )(qk, qk, p_ml, p_ml, gc, gc, gr, gr, gbc, gbr, cn_in, m_in)


def _fourier_kernel(ct_ref, st_ref, a_ref, b_ref, o_ref):
    @pl.when(pl.program_id(1) == 0)
    def _():
        o_ref[...] = jnp.zeros_like(o_ref)
    o_ref[...] += _dot(ct_ref[...], a_ref[...]) + _dot(st_ref[...], b_ref[...])


def _fourier_call(ct, mst, fa, fb):
    t, n = fa.shape
    tm = min(t, 512)
    return pl.pallas_call(
        _fourier_kernel,
        out_shape=jax.ShapeDtypeStruct((t, n), F32),
        grid=(t // tm, t // tm),
        in_specs=[pl.BlockSpec((tm, tm), lambda i, k: (i, k)),
                  pl.BlockSpec((tm, tm), lambda i, k: (i, k)),
                  pl.BlockSpec((tm, n), lambda i, k: (k, 0)),
                  pl.BlockSpec((tm, n), lambda i, k: (k, 0))],
        out_specs=pl.BlockSpec((tm, n), lambda i, k: (i, 0)),
        compiler_params=_cparams("arbitrary", "arbitrary"),
        name="fourier_pos",
    )(ct, mst, fa, fb)


def _merge_kernel(x_ref, m2_ref, m3_ref, m4_ref, hf_ref, hb_ref, o_ref, zgm_ref, cv_ref, zft_ref, gate_ref,
                  mlg_ref, cvg_ref, cvb_ref, n2g_ref, wb0_ref, wb1_ref, wb2_ref, wb3_ref, wout_ref,
                  xo_ref, h2_ref):
    hsum = hf_ref[0] + hb_ref[0]
    parts = []
    for hd in range(ML_HEADS):
        blk = hsum[:, hd * HEAD_PAD:(hd + 1) * HEAD_PAD]
        ms = jnp.sum(blk * blk, axis=-1, keepdims=True) * (1.0 / ML_DH)
        parts.append(blk * lax.rsqrt(ms + EPS))
    z_ml = jnp.concatenate(parts, axis=1) * mlg_ref[...] * _sigmoid(o_ref[0])
    z_cv = _silu(_layernorm(cv_ref[0], cvg_ref[...], cvb_ref[...]))

    def gate(i):
        return gate_ref[0, :, i * D_MODEL:(i + 1) * D_MODEL].astype(F32)

    y = gate(0) * _dot(z_ml.astype(BF16), wb0_ref[...])
    y = y + gate(1) * _dot(zgm_ref[0].astype(BF16), wb1_ref[...])
    y = y + gate(2) * _dot(z_cv.astype(BF16), wb2_ref[...])
    y = y + gate(3) * _dot(zft_ref[...].astype(BF16), wb3_ref[...])
    xo = x_ref[0] + m2_ref[0, 0] * _dot(y.astype(BF16), wout_ref[...])
    xo_ref[0] = xo
    h2_ref[0] = _rms_mod(xo, n2g_ref[...], m3_ref[0, 0], m4_ref[0, 0])


def _merge_call(x, mod, row_fn, hf, hb, p_ml, zgm, cv, zft, gate, mlg, cvg, cvb, n2g, wb0, wb1, wb2, wb3, wout):
    b, t, d = x.shape
    tm = 256
    tok = lambda w: pl.BlockSpec((1, tm, w), lambda i, j: (i, j, 0))
    rf = lambda i, j: row_fn(i)
    return pl.pallas_call(
        _merge_kernel,
        out_shape=(jax.ShapeDtypeStruct((b, t, d), F32), jax.ShapeDtypeStruct((b, t, d), F32)),
        grid=(b, t // tm),
        in_specs=[tok(d), _mod_spec(rf, 2), _mod_spec(rf, 3), _mod_spec(rf, 4),
                  tok(ML_PAD_W), tok(ML_PAD_W),
                  pl.BlockSpec((1, tm, ML_PAD_W), lambda i, j: (i, j, 3)),
                  tok(MIX_W), tok(MIX_W),
                  pl.BlockSpec((tm, MIX_W), lambda i, j: (j, i)),
                  tok(4 * d),
                  _const_spec((1, ML_PAD_W)), _const_spec((1, MIX_W)), _const_spec((1, MIX_W)), _const_spec((1, d)),
                  _const_spec(wb0.shape), _const_spec(wb1.shape), _const_spec(wb2.shape), _const_spec(wb3.shape),
                  _const_spec(wout.shape)],
        out_specs=(tok(d), tok(d)),
        compiler_params=_cparams("arbitrary", "arbitrary"),
        name="merge",
    )(x, mod, mod, mod, hf, hb, p_ml, zgm, cv, zft, gate, mlg, cvg, cvb, n2g, wb0, wb1, wb2, wb3, wout)


def _top16_rows(s, ids):
    big = jnp.int32(2 ** 30)
    out_rows = lax.broadcasted_iota(jnp.int32, (PEER_TOPK, LANE), 0)
    vals = jnp.zeros((PEER_TOPK, LANE), F32)
    sel = jnp.zeros((PEER_TOPK, LANE), jnp.int32)
    for i in range(PEER_TOPK):
        m = jnp.max(s, axis=0, keepdims=True)
        pick = jnp.min(jnp.where(s == m, ids, big), axis=0, keepdims=True)
        s = jnp.where(ids == pick, -jnp.inf, s)
        vals = jnp.where(out_rows == i, m, vals)
        sel = jnp.where(out_rows == i, pick, sel)
    return vals, sel


def _peer_stage2(v1, i1, v2, i2):
    big = jnp.int32(2 ** 30)
    sub = lax.broadcasted_iota(jnp.int32, (8, LANE), 0)
    tiles = []
    for a in range(8):
        va = v1[a:a + 1, :]
        ea = i1[a:a + 1, :] * N_KEYS
        for half in range(2 if a == 0 else 1):
            b = sub + 8 * half
            ok = (a + 1) * (b + 1) <= PEER_TOPK
            sc = jnp.where(ok, va + v2[8 * half:8 * half + 8, :], -jnp.inf)
            tiles.append((sc, a * PEER_TOPK + b, ea + i2[8 * half:8 * half + 8, :]))
    tiles.append((v1[8:16, :] + v2[0:1, :], (sub + 8) * PEER_TOPK, i1[8:16, :] * N_KEYS + i2[0:1, :]))
    sc = jnp.concatenate([t[0] for t in tiles], axis=0)
    pos = jnp.concatenate([jnp.broadcast_to(t[1], (8, LANE)) for t in tiles], axis=0)
    eid = jnp.concatenate([t[2] for t in tiles], axis=0)
    out_rows = lax.broadcasted_iota(jnp.int32, (PEER_TOPK, LANE), 0)
    vals = jnp.zeros((PEER_TOPK, LANE), F32)
    sel = jnp.zeros((PEER_TOPK, LANE), jnp.int32)
    for i in range(PEER_TOPK):
        m = jnp.max(sc, axis=0, keepdims=True)
        pick = jnp.min(jnp.where(sc == m, pos, big), axis=0, keepdims=True)
        hit = pos == pick
        e = jnp.max(jnp.where(hit, eid, -1), axis=0, keepdims=True)
        sc = jnp.where(hit, -jnp.inf, sc)
        vals = jnp.where(out_rows == i, m, vals)
        sel = jnp.where(out_rows == i, e, sel)
    return vals, sel


def _route_kernel(h_ref, wq_ref, keys_ref, e_ref, g_ref):
    tm = h_ref.shape[0]
    q = _dot(h_ref[...].astype(BF16), wq_ref[...])
    key_ids = lax.broadcasted_iota(jnp.int32, (N_KEYS, LANE), 0)
    for c in range(tm // LANE):
        for hd in range(PEER_HEADS):
            tops = []
            for p in range(2):
                lo = (hd * 2 + p) * N_KEYS
                qs = q[c * LANE:(c + 1) * LANE, lo:lo + N_KEYS].astype(BF16)
                s = _dot_nt(keys_ref[p], qs)
                tops.append(_top16_rows(s, key_ids))
            sc, eid = _peer_stage2(tops[0][0], tops[0][1], tops[1][0], tops[1][1])
            w = jnp.exp(sc - sc[0:1, :])
            w = w / jnp.sum(w, axis=0, keepdims=True)
            e_ref[hd * PEER_TOPK:(hd + 1) * PEER_TOPK, c * LANE:(c + 1) * LANE] = eid
            g_ref[hd * PEER_TOPK:(hd + 1) * PEER_TOPK, c * LANE:(c + 1) * LANE] = w


def _route_call(h2, wq, keys):
    n, d = h2.shape
    tm = 256
    return pl.pallas_call(
        _route_kernel,
        out_shape=(jax.ShapeDtypeStruct((PEER_SEL, n), jnp.int32), jax.ShapeDtypeStruct((PEER_SEL, n), F32)),
        grid=(n // tm,),
        in_specs=[pl.BlockSpec((tm, d), lambda i: (i, 0)), _const_spec(wq.shape), _const_spec(keys.shape)],
        out_specs=(pl.BlockSpec((PEER_SEL, tm), lambda i: (0, i)), pl.BlockSpec((PEER_SEL, tm), lambda i: (0, i))),
        compiler_params=_cparams("arbitrary"),
        name="peer_route",
    )(h2, wq, keys)


PEER_TB = 64
PEER_NBUF = 4


def _experts_kernel(idx_ref, g_ref, h_ref, x_ref, m5_ref, fg_ref, uv_ref, o_ref, buf, sem, *, final):
    tb = h_ref.shape[0]
    d = D_MODEL

    def gather(t, slot):
        for r in range(PEER_SEL):
            pltpu.make_async_copy(uv_ref.at[pl.ds(idx_ref[t, r], 1)], buf.at[slot, pl.ds(r, 1)], sem.at[slot]).start()

    def wait(slot):
        pltpu.make_async_copy(uv_ref.at[pl.ds(0, PEER_SEL)], buf.at[slot], sem.at[slot]).wait()

    eye = (lax.broadcasted_iota(jnp.int32, (PEER_SEL, PEER_SEL), 0)
           == lax.broadcasted_iota(jnp.int32, (PEER_SEL, PEER_SEL), 1))

    def compute(t, slot):
        hrow = h_ref[t]
        s = jnp.sum(buf[slot, :, 0:d] * hrow, axis=1, keepdims=True)
        gcol = jnp.sum(jnp.where(eye, g_ref[t], 0.0), axis=1, keepdims=True)
        a = _gelu(s) * gcol
        o = jnp.sum(buf[slot, :, d:2 * d] * a, axis=0, keepdims=True)
        xo = x_ref[t] + m5_ref[0, 0] * o
        if final:
            xo = xo * lax.rsqrt(jnp.mean(xo * xo, axis=-1, keepdims=True) + EPS) * fg_ref[...]
        o_ref[t] = xo

    for slot in range(PEER_NBUF):
        gather(slot, slot)

    def body(i, carry):
        for slot in range(PEER_NBUF):
            t = i * PEER_NBUF + slot
            wait(slot)
            compute(t, slot)

            @pl.when(t + PEER_NBUF < tb)
            def _():
                gather(t + PEER_NBUF, slot)
        return carry

    lax.fori_loop(0, tb // PEER_NBUF, body, 0)


def _experts_call(idx, g, h2, x, mod, row_of_block, final_g, uv, final):
    n = h2.shape[0]
    tb = PEER_TB
    d = D_MODEL
    row3 = lambda a: a.reshape(n, 1, a.shape[-1])
    tok = lambda w: pl.BlockSpec((tb, 1, w), lambda i: (i, 0, 0))
    out = pl.pallas_call(
        functools.partial(_experts_kernel, final=final),
        out_shape=jax.ShapeDtypeStruct((n, 1, d), F32),
        grid=(n // tb,),
        in_specs=[pl.BlockSpec((tb, PEER_SEL), lambda i: (i, 0), memory_space=pltpu.SMEM),
                  tok(PEER_SEL), tok(d), tok(d),
                  _mod_spec(lambda i: row_of_block(i), 5),
                  _const_spec((1, d)),
                  pl.BlockSpec(memory_space=pl.ANY)],
        out_specs=tok(d),
        scratch_shapes=[pltpu.VMEM((PEER_NBUF, PEER_SEL, 2 * d), F32), pltpu.SemaphoreType.DMA((PEER_NBUF,))],
        compiler_params=_cparams("arbitrary"),
        name="peer_experts_final" if final else "peer_experts",
    )(idx, row3(g), row3(h2), row3(x), mod, final_g, uv)
    return out.reshape(n, d)


def _grid_sincos(n_tok, d):
    rows = n_tok // GRID_W
    n_freq = d // 4
    freq = 1.0 / (10000.0 ** (jnp.arange(n_freq, dtype=F32) / n_freq))
    r = jnp.repeat(jnp.arange(rows, dtype=F32), GRID_W)
    cc = jnp.tile(jnp.arange(GRID_W, dtype=F32), rows)
    ar = r[:, None] * freq[None, :]
    ac = cc[:, None] * freq[None, :]
    return jnp.concatenate([jnp.sin(ar), jnp.cos(ar), jnp.sin(ac), jnp.cos(ac)], axis=-1)


def _dft_tables(t_len):
    i = jnp.arange(t_len, dtype=jnp.int32)
    ph = (i[:, None] * i[None, :]) % t_len
    ang = ph.astype(F32) * (2.0 * math.pi / t_len)
    scale = 1.0 / math.sqrt(t_len * FT_DG)
    return (jnp.cos(ang) * scale).astype(BF16), (-jnp.sin(ang) * scale).astype(BF16)


def _channel_dft():
    i = jnp.arange(MIX_W, dtype=jnp.int32)
    same = (i[:, None] // FT_DG) == (i[None, :] // FT_DG)
    ph = ((i[:, None] % FT_DG) * (i[None, :] % FT_DG)) % FT_DG
    ang = ph.astype(F32) * (2.0 * math.pi / FT_DG)
    c = jnp.where(same, jnp.cos(ang), 0.0)
    s = jnp.where(same, jnp.sin(ang), 0.0)
    return jnp.concatenate([c, s], axis=1).astype(BF16)


def _pad_heads(a):
    lead = a.shape[:-1]
    a = a.reshape(lead + (ML_HEADS, ML_DH))
    a = jnp.pad(a, [(0, 0)] * len(lead) + [(0, 0), (0, HEAD_PAD - ML_DH)])
    return a.reshape(lead + (ML_PAD_W,))


def _layer_weights(l, w_in, ml_conv_w, ml_conv_b, ml_gate_b, ml_norm_g, gm_w_s, gm_b_s, w_branch):
    w = w_in[l]
    q, k, v = w[:, 0:256], w[:, 256:512], w[:, 512:768]
    gates, o = w[:, 768:784], w[:, 784:1040]
    w_ml = jnp.concatenate([_pad_heads(q), _pad_heads(k), _pad_heads(v), _pad_heads(o)], axis=1).astype(BF16)
    w_gc = jnp.pad(gates, ((0, 0), (0, LANE - 16))).astype(BF16)
    w_gr = gates.T.astype(BF16)
    w_rest = w[:, 1040:].astype(BF16)
    cw = ml_conv_w[l]
    conv_w = jnp.concatenate([_pad_heads(cw[:, :256]), _pad_heads(cw[:, 256:])], axis=1)
    cb = ml_conv_b[l]
    conv_b = jnp.concatenate([_pad_heads(cb[:256]), _pad_heads(cb[256:])])[None, :]
    gb = ml_gate_b[l]
    gbc = jnp.pad(gb, (0, LANE - 16))[None, :]
    gbr = jnp.broadcast_to(gb[:, None], (16, ML_CHUNK))
    mlg = _pad_heads(ml_norm_g[l].reshape(-1))[None, :]
    wcat = jnp.transpose(gm_w_s[l], (1, 0, 2)).reshape(GM_CHUNK, GM_GROUPS * GM_CHUNK).astype(BF16)
    bias_gm = jnp.repeat(gm_b_s[l].T, GM_DG, axis=1)
    wb = w_branch[l]
    wb0 = jnp.pad(wb[0].reshape(ML_HEADS, ML_DH, D_MODEL), ((0, 0), (0, HEAD_PAD - ML_DH), (0, 0)))
    wb0 = wb0.reshape(ML_PAD_W, D_MODEL).astype(BF16)
    return dict(w_ml=w_ml, w_gc=w_gc, w_gr=w_gr, w_rest=w_rest, conv_w=conv_w, conv_b=conv_b, gbc=gbc, gbr=gbr,
                mlg=mlg, wcat=wcat, bias_gm=bias_gm, wb0=wb0,
                wb1=wb[1].astype(BF16), wb2=wb[2].astype(BF16), wb3=wb[3].astype(BF16))


def kernel(x, c, ctx, c_ctx, w_ada, b_ada, norm1_g, norm2_g, w_in, ml_conv_w, ml_conv_b, ml_gate_b, ml_norm_g,
           gm_ln_g, gm_ln_b, gm_w_s, gm_b_s, cv_dw_w, cv_dw_b, cv_ln_g, cv_ln_b, w_branch, w_out, peer_w_q,
           peer_keys, peer_u, peer_v, final_norm_g):
    b_sz, n_tok, d = x.shape
    n_ctx = ctx.shape[1]
    depth = w_ada.shape[0]
    ctx_row = b_sz

    cond = jnp.zeros((16, d), F32).at[:b_sz].set(c).at[ctx_row].set(c_ctx)
    mod_all = _ada_call(cond, w_ada, b_ada)
    x = _addpos_call(x, _grid_sincos(n_tok, d))
    xc = ctx
    cs = _channel_dft()
    dft_x = _dft_tables(n_tok)
    dft_c = _dft_tables(n_ctx)
    final_g = final_norm_g[None, :]
    lat_row = lambda i: i
    ctx_rowf = lambda i: ctx_row

    def mixers(xs, l, lw, mod, row_fn, dft, cn, m, need_out):
        p_ml, gc, gr = _inproj_ml_call(xs, mod, row_fn, norm1_g[l][None], lw["w_ml"], lw["w_gc"], lw["w_gr"])
        qk = _dwconv_call(p_ml, 0, 2 * ML_PAD_W, lw["conv_w"], lw["conv_b"], True)
        res = _mlstm_call(qk, p_ml, gc, gr, lw["gbc"], lw["gbr"], cn, m, need_out)
        if not need_out:
            return None, res[0], res[1]
        hf, hb, cn, m = res
        zgm, zglu, fa, fb, gate = _inproj_rest_call(xs, mod, row_fn, norm1_g[l][None], lw["w_rest"],
                                                    gm_ln_g[l][None], gm_ln_b[l][None], lw["wcat"],
                                                    lw["bias_gm"], cs)
        cv = _dwconv_call(zglu, 0, MIX_W, cv_dw_w[l], cv_dw_b[l][None], False)
        zft = _fourier_call(dft[0], dft[1], fa, fb)
        xo, h2 = _merge_call(xs, mod, row_fn, hf, hb, p_ml, zgm, cv, zft, gate, lw["mlg"], cv_ln_g[l][None],
                             cv_ln_b[l][None], norm2_g[l][None], lw["wb0"], lw["wb1"], lw["wb2"], lw["wb3"],
                             w_out[l].astype(BF16))
        return (xo, h2), cn, m

    def peer(xo, h2, l, mod, row_of_block, uv, final):
        bsz, t, _ = xo.shape
        n = bsz * t
        h2f = h2.reshape(n, d)
        e_t, g_t = _route_call(h2f, peer_w_q[l].astype(BF16), peer_keys[l].astype(BF16))
        out = _experts_call(e_t.T, g_t.T, h2f, xo.reshape(n, d), mod, row_of_block, final_g, uv, final)
        return out.reshape(bsz, t, d)

    for l in range(depth):
        last = l == depth - 1
        lw = _layer_weights(l, w_in, ml_conv_w, ml_conv_b, ml_gate_b, ml_norm_g, gm_w_s, gm_b_s, w_branch)
        mod = mod_all[l].reshape(16, 6, 1, d)
        uv = jnp.concatenate([peer_u[l], peer_v[l]], axis=1)
        cn0 = jnp.zeros((b_sz, 2 * ML_HEADS, HEAD_PAD, HEAD_PAD), F32)
        m0 = jnp.zeros((b_sz, 2 * ML_HEADS, 1, LANE), F32)
        res_c, cn, m = mixers(xc, l, lw, mod, ctx_rowf, dft_c, cn0, m0, not last)
        res_x, _, _ = mixers(x, l, lw, mod, lat_row, dft_x, cn, m, True)
        blocks_per_batch = n_tok // PEER_TB
        x = peer(res_x[0], res_x[1], l, mod, lambda i: i // blocks_per_batch, uv, last)
        if not last:
            xc = peer(res_c[0], res_c[1], l, mod, lambda i: ctx_row, uv, False)
    return x
```

```python
import functools
import math

import jax
import jax.numpy as jnp
from jax import lax
from jax.experimental import pallas as pl
from jax.experimental.pallas import tpu as pltpu

F32 = jnp.float32
BF16 = jnp.bfloat16
HIGHEST = lax.Precision.HIGHEST

D_MODEL = 1024
GRID_W = 64
EPS = 1e-6
MIX_W = 256
ML_HEADS = 4
ML_DH = 64
ML_CHUNK = 128
GM_CHUNK = 128
GM_GROUPS = 4
GM_DG = 64
CV_K = 31
ML_CONV = 3
FT_GROUPS = 4
FT_DG = 64
PEER_HEADS = 8
PEER_TOPK = 16
N_KEYS = 128
PEER_SEL = PEER_HEADS * PEER_TOPK

LANE = 128
HEAD_PAD = LANE
ML_PAD_W = ML_HEADS * HEAD_PAD
CONV_HALO = 16
VMEM_LIMIT = 56 * 1024 * 1024


def _cparams(*sem):
    return pltpu.CompilerParams(dimension_semantics=sem, vmem_limit_bytes=VMEM_LIMIT)


def _sigmoid(x):
    return 1.0 / (1.0 + jnp.exp(-x))


def _silu(x):
    return x * _sigmoid(x)


def _gelu(x):
    return 0.5 * x * (1.0 + jnp.tanh(math.sqrt(2.0 / math.pi) * (x + 0.044715 * (x * x * x))))


def _log_sigmoid(x):
    return jnp.minimum(x, 0.0) - jnp.log(1.0 + jnp.exp(-jnp.abs(x)))


def _rms_mod(x, g, shift, scale):
    y = x * lax.rsqrt(jnp.mean(x * x, axis=-1, keepdims=True) + EPS) * g
    return y * (1.0 + scale) + shift


def _layernorm(x, g, b):
    mu = jnp.mean(x, axis=-1, keepdims=True)
    xc = x - mu
    return xc * lax.rsqrt(jnp.mean(xc * xc, axis=-1, keepdims=True) + EPS) * g + b


def _dot(a, b):
    return jnp.dot(a, b, preferred_element_type=F32)


def _dot_nt(a, b):
    return lax.dot_general(a, b, (((1,), (1,)), ((), ())), preferred_element_type=F32)


def _dot_exact(a, b):
    return jnp.dot(a, b, preferred_element_type=F32, precision=HIGHEST)


def _ada_kernel(s_ref, w_ref, b_ref, o_ref):
    s = _silu(s_ref[...])
    o_ref[0] = _dot_exact(s, w_ref[0]) + b_ref[0]


def _ada_call(cond, w_ada, b_ada):
    depth, d, n6 = w_ada.shape
    rows = cond.shape[0]
    tn = 1536
    return pl.pallas_call(
        _ada_kernel,
        out_shape=jax.ShapeDtypeStruct((depth, rows, n6), F32),
        grid=(depth, n6 // tn),
        in_specs=[pl.BlockSpec((rows, d), lambda l, j: (0, 0)),
                  pl.BlockSpec((1, d, tn), lambda l, j: (l, 0, j)),
                  pl.BlockSpec((1, 1, tn), lambda l, j: (l, 0, j))],
        out_specs=pl.BlockSpec((1, rows, tn), lambda l, j: (l, 0, j)),
        compiler_params=_cparams("arbitrary", "arbitrary"),
        name="ada_mod",
    )(cond, w_ada, b_ada.reshape(depth, 1, n6))


def _addpos_kernel(x_ref, p_ref, o_ref):
    o_ref[0] = x_ref[0] + p_ref[...]


def _addpos_call(x, pos):
    b, t, d = x.shape
    tm = min(t, 512)
    return pl.pallas_call(
        _addpos_kernel,
        out_shape=jax.ShapeDtypeStruct(x.shape, F32),
        grid=(t // tm, b),
        in_specs=[pl.BlockSpec((1, tm, d), lambda i, j: (j, i, 0)),
                  pl.BlockSpec((tm, d), lambda i, j: (i, 0))],
        out_specs=pl.BlockSpec((1, tm, d), lambda i, j: (j, i, 0)),
        compiler_params=_cparams("arbitrary", "arbitrary"),
        name="add_pos",
    )(x, pos)


def _mod_spec(row_fn, k):
    return pl.BlockSpec((1, 1, 1, D_MODEL), lambda *g: (row_fn(*g), k, 0, 0))


def _const_spec(shape):
    nd = len(shape)
    return pl.BlockSpec(shape, lambda *g: (0,) * nd)


def _inproj_ml_kernel(x_ref, sh_ref, sc_ref, g_ref, w_ref, wgc_ref, wgr_ref, p_ref, gc_ref, gr_ref):
    h = _rms_mod(x_ref[0], g_ref[...], sh_ref[0, 0], sc_ref[0, 0]).astype(BF16)
    p_ref[0] = _dot(h, w_ref[...])
    gc_ref[0] = _dot(h, wgc_ref[...])
    gr_ref[0] = _dot_nt(wgr_ref[...], h)


def _inproj_ml_call(x, mod, row_fn, norm_g, w_ml, w_gc, w_gr):
    b, t, d = x.shape
    tm = 256
    n_ml = w_ml.shape[1]
    return pl.pallas_call(
        _inproj_ml_kernel,
        out_shape=(jax.ShapeDtypeStruct((b, t, n_ml), F32),
                   jax.ShapeDtypeStruct((b, t, LANE), F32),
                   jax.ShapeDtypeStruct((b, 16, t), F32)),
        grid=(b, t // tm),
        in_specs=[pl.BlockSpec((1, tm, d), lambda i, j: (i, j, 0)),
                  _mod_spec(lambda i, j: row_fn(i), 0),
                  _mod_spec(lambda i, j: row_fn(i), 1),
                  _const_spec((1, d)),
                  _const_spec(w_ml.shape),
                  _const_spec(w_gc.shape),
                  _const_spec(w_gr.shape)],
        out_specs=(pl.BlockSpec((1, tm, n_ml), lambda i, j: (i, j, 0)),
                   pl.BlockSpec((1, tm, LANE), lambda i, j: (i, j, 0)),
                   pl.BlockSpec((1, 16, tm), lambda i, j: (i, 0, j))),
        compiler_params=_cparams("arbitrary", "arbitrary"),
        name="inproj_ml",
    )(x, mod, mod, norm_g, w_ml, w_gc, w_gr)


def _inproj_rest_kernel(x_ref, sh_ref, sc_ref, g_ref, w_ref, lng_ref, lnb_ref, wcat_ref, bgm_ref, cs_ref,
                        zgm_ref, zglu_ref, fa_ref, fb_ref, gate_ref):
    tm = x_ref.shape[1]
    h = _rms_mod(x_ref[0], g_ref[...], sh_ref[0, 0], sc_ref[0, 0]).astype(BF16)
    z = _gelu(_dot(h, w_ref[:, 0:2 * MIX_W]))
    u = z[:, :MIX_W]
    v = _layernorm(z[:, MIX_W:], lng_ref[...], lnb_ref[...])
    grp = lax.broadcasted_iota(jnp.int32, (GM_CHUNK, MIX_W), 1) // GM_DG
    for j in range(tm // GM_CHUNK):
        vc = v[j * GM_CHUNK:(j + 1) * GM_CHUNK]
        vstack = jnp.concatenate([jnp.where(grp == g, vc, 0.0) for g in range(GM_GROUPS)], axis=0).astype(BF16)
        s = _dot(wcat_ref[...], vstack) + bgm_ref[...]
        zgm_ref[0, j * GM_CHUNK:(j + 1) * GM_CHUNK, :] = u[j * GM_CHUNK:(j + 1) * GM_CHUNK] * s
    pc = _dot(h, w_ref[:, 2 * MIX_W:4 * MIX_W])
    zglu_ref[0] = pc[:, :MIX_W] * _sigmoid(pc[:, MIX_W:])
    pf = _dot(h, w_ref[:, 4 * MIX_W:5 * MIX_W]).astype(BF16)
    ab = _dot(pf, cs_ref[...])
    fa_ref[...] = ab[:, :MIX_W].astype(BF16)
    fb_ref[...] = ab[:, MIX_W:].astype(BF16)
    for i in range(4):
        lo = 5 * MIX_W + i * D_MODEL
        gate_ref[0, :, i * D_MODEL:(i + 1) * D_MODEL] = _sigmoid(_dot(h, w_ref[:, lo:lo + D_MODEL])).astype(BF16)


def _inproj_rest_call(x, mod, row_fn, norm_g, w_rest, gm_ln_g, gm_ln_b, wcat, bias_gm, cs):
    b, t, d = x.shape
    tm = 256
    return pl.pallas_call(
        _inproj_rest_kernel,
        out_shape=(jax.ShapeDtypeStruct((b, t, MIX_W), F32),
                   jax.ShapeDtypeStruct((b, t, MIX_W), F32),
                   jax.ShapeDtypeStruct((t, b * MIX_W), BF16),
                   jax.ShapeDtypeStruct((t, b * MIX_W), BF16),
                   jax.ShapeDtypeStruct((b, t, 4 * D_MODEL), BF16)),
        grid=(b, t // tm),
        in_specs=[pl.BlockSpec((1, tm, d), lambda i, j: (i, j, 0)),
                  _mod_spec(lambda i, j: row_fn(i), 0),
                  _mod_spec(lambda i, j: row_fn(i), 1),
                  _const_spec((1, d)),
                  _const_spec(w_rest.shape),
                  _const_spec((1, MIX_W)), _const_spec((1, MIX_W)),
                  _const_spec(wcat.shape), _const_spec(bias_gm.shape), _const_spec(cs.shape)],
        out_specs=(pl.BlockSpec((1, tm, MIX_W), lambda i, j: (i, j, 0)),
                   pl.BlockSpec((1, tm, MIX_W), lambda i, j: (i, j, 0)),
                   pl.BlockSpec((tm, MIX_W), lambda i, j: (j, i)),
                   pl.BlockSpec((tm, MIX_W), lambda i, j: (j, i)),
                   pl.BlockSpec((1, tm, 4 * D_MODEL), lambda i, j: (i, j, 0))),
        compiler_params=_cparams("arbitrary", "arbitrary"),
        name="inproj_rest",
    )(x, mod, mod, norm_g, w_rest, gm_ln_g, gm_ln_b, wcat, bias_gm, cs)


def _dwconv_kernel(x_ref, prev_ref, next_ref, w_ref, b_ref, o_ref, win_ref, *, taps, act):
    i = pl.program_id(2)
    last = pl.num_programs(2) - 1
    rc = x_ref.shape[1]
    win_ref[0:CONV_HALO, :] = jnp.where(i > 0, prev_ref[0], 0.0)
    win_ref[CONV_HALO:CONV_HALO + rc, :] = x_ref[0]
    win_ref[CONV_HALO + rc:, :] = jnp.where(i < last, next_ref[0], 0.0)
    sub = 128
    for r in range(rc // sub):
        acc = jnp.broadcast_to(b_ref[...], (sub, LANE))
        for j in range(taps):
            lo = CONV_HALO + r * sub + j - taps // 2
            acc = acc + w_ref[j:j + 1, :] * win_ref[lo:lo + sub, :]
        if act:
            acc = _silu(acc)
        o_ref[0, r * sub:(r + 1) * sub, :] = acc


def _dwconv_call(x, col0, ncol, w, bias, act):
    b, t, _ = x.shape
    taps = w.shape[0]
    rc = min(t, 512)
    c0 = col0 // LANE
    hb = rc // CONV_HALO
    nh = t // CONV_HALO
    return pl.pallas_call(
        functools.partial(_dwconv_kernel, taps=taps, act=act),
        out_shape=jax.ShapeDtypeStruct((b, t, ncol), F32),
        grid=(b, ncol // LANE, t // rc),
        in_specs=[pl.BlockSpec((1, rc, LANE), lambda bi, c, i: (bi, i, c0 + c)),
                  pl.BlockSpec((1, CONV_HALO, LANE), lambda bi, c, i: (bi, jnp.maximum(i * hb - 1, 0), c0 + c)),
                  pl.BlockSpec((1, CONV_HALO, LANE), lambda bi, c, i: (bi, jnp.minimum((i + 1) * hb, nh - 1), c0 + c)),
                  pl.BlockSpec((taps, LANE), lambda bi, c, i: (0, c)),
                  pl.BlockSpec((1, LANE), lambda bi, c, i: (0, c))],
        out_specs=pl.BlockSpec((1, rc, LANE), lambda bi, c, i: (bi, i, c)),
        scratch_shapes=[pltpu.VMEM((rc + 2 * CONV_HALO, LANE), F32)],
        compiler_params=_cparams("arbitrary", "arbitrary", "arbitrary"),
        name="dwconv_k%d" % taps,
    )(x, x, x, w, bias)


def _mlstm_kernel(*refs, need_out):
    (qkf_ref, qkb_ref, vf_ref, vb_ref, gcf_ref, gcb_ref, grf_ref, grb_ref, gbc_ref, gbr_ref,
     cn_in_ref, m_in_ref) = refs[:12]
    if need_out:
        hf_ref, hb_ref, cn_out_ref, m_out_ref, cn_s, m_s = refs[12:]
    else:
        cn_out_ref, m_out_ref, cn_s, m_s = refs[12:]
        hf_ref = hb_ref = None
    j = pl.program_id(1)
    last = pl.num_programs(1) - 1
    L = ML_CHUNK

    @pl.when(j == 0)
    def _():
        cn_s[...] = cn_in_ref[0]
        m_s[...] = m_in_ref[0]

    row = lax.broadcasted_iota(jnp.int32, (L, L), 0)
    col = lax.broadcasted_iota(jnp.int32, (L, L), 1)
    lane = lax.broadcasted_iota(jnp.int32, (1, HEAD_PAD), 1)
    one_at_dh = (lane == ML_DH).astype(F32)
    keep_dh = (lane < ML_DH).astype(F32)

    for direction in range(2):
        qk_ref, v_ref, gc_ref, gr_ref, h_ref = ((qkf_ref, vf_ref, gcf_ref, grf_ref, hf_ref) if direction == 0
                                                else (qkb_ref, vb_ref, gcb_ref, grb_ref, hb_ref))
        visible = (col <= row) if direction == 0 else (col >= row)
        tri_c = visible.astype(F32)
        tri_r = ((row <= col) if direction == 0 else (row >= col)).astype(F32)
        gcv = gc_ref[0] + gbc_ref[...]
        grv = gr_ref[0] + gbr_ref[...]
        lf_c = _log_sigmoid(gcv)
        lf_r = _log_sigmoid(grv)
        cum_c = _dot_exact(tri_c, lf_c)
        cum_r = _dot_exact(lf_r, tri_r)
        tot_c = jnp.sum(lf_c, axis=0, keepdims=True)
        for hd in range(ML_HEADS):
            g = direction * ML_HEADS + hd
            gi_col = g
            gf_col = 2 * ML_HEADS + g
            li_r = grv[gi_col:gi_col + 1, :]
            li_c = gcv[:, gi_col:gi_col + 1]
            cc = cum_c[:, gf_col:gf_col + 1]
            cr = cum_r[gf_col:gf_col + 1, :]
            tot = tot_c[:, gf_col:gf_col + 1]
            m_prev = m_s[g][:, 0:1]
            cn = cn_s[g]
            qh = qk_ref[0, :, hd * HEAD_PAD:(hd + 1) * HEAD_PAD]
            kh = qk_ref[0, :, ML_PAD_W + hd * HEAD_PAD:ML_PAD_W + (hd + 1) * HEAD_PAD] * (ML_DH ** -0.5)
            v1 = v_ref[0, :, hd * HEAD_PAD:(hd + 1) * HEAD_PAD] + one_at_dh
            w_r = tot - cr + li_r
            w_c = tot - cc + li_c
            m_new = jnp.maximum(tot + m_prev, jnp.max(w_r, axis=1, keepdims=True))
            ws_c = jnp.exp(w_c - m_new)
            decay = jnp.exp(tot + m_prev - m_new)
            if need_out:
                dlog = jnp.where(visible, cc - cr + li_r, -jnp.inf)
                inter = cc + m_prev
                m_t = jnp.maximum(inter, jnp.max(dlog, axis=1, keepdims=True))
                s = _dot_nt(qh.astype(BF16), kh.astype(BF16)) * jnp.exp(dlog - m_t)
                gi = jnp.exp(inter - m_t)
                nd = gi * _dot(qh.astype(BF16), cn.astype(BF16)) + _dot(s.astype(BF16), v1.astype(BF16))
                den = nd[:, ML_DH:ML_DH + 1]
                hh = nd / jnp.maximum(jnp.abs(den), jnp.exp(-m_t))
                h_ref[0, :, hd * HEAD_PAD:(hd + 1) * HEAD_PAD] = hh * keep_dh
            kw = (kh * ws_c).astype(BF16)
            upd = lax.dot_general(kw, v1.astype(BF16), (((0,), (0,)), ((), ())), preferred_element_type=F32)
            cn_s[g] = decay * cn + upd
            m_s[g] = jnp.broadcast_to(m_new, (1, LANE))

    @pl.when(j == last)
    def _():
        cn_out_ref[0] = cn_s[...]
        m_out_ref[0] = m_s[...]


def _mlstm_call(qk, p_ml, gc, gr, gbc, gbr, cn_in, m_in, need_out):
    b, t, _ = qk.shape
    L = ML_CHUNK
    nc = t // L
    g2 = 2 * ML_HEADS
    fwd = lambda i, j: (i, j, 0)
    bwd = lambda i, j: (i, nc - 1 - j, 0)
    in_specs = [pl.BlockSpec((1, L, 2 * ML_PAD_W), fwd),
                pl.BlockSpec((1, L, 2 * ML_PAD_W), bwd),
                pl.BlockSpec((1, L, ML_PAD_W), lambda i, j: (i, j, 2)),
                pl.BlockSpec((1, L, ML_PAD_W), lambda i, j: (i, nc - 1 - j, 2)),
                pl.BlockSpec((1, L, LANE), fwd),
                pl.BlockSpec((1, L, LANE), bwd),
                pl.BlockSpec((1, 16, L), lambda i, j: (i, 0, j)),
                pl.BlockSpec((1, 16, L), lambda i, j: (i, 0, nc - 1 - j)),
                _const_spec((1, LANE)),
                _const_spec((16, L)),
                pl.BlockSpec((1, g2, HEAD_PAD, HEAD_PAD), lambda i, j: (i, 0, 0, 0)),
                pl.BlockSpec((1, g2, 1, LANE), lambda i, j: (i, 0, 0, 0))]
    state_shapes = (jax.ShapeDtypeStruct((b, g2, HEAD_PAD, HEAD_PAD), F32),
                    jax.ShapeDtypeStruct((b, g2, 1, LANE), F32))
    state_specs = (pl.BlockSpec((1, g2, HEAD_PAD, HEAD_PAD), lambda i, j: (i, 0, 0, 0)),
                   pl.BlockSpec((1, g2, 1, LANE), lambda i, j: (i, 0, 0, 0)))
    if need_out:
        out_shape = (jax.ShapeDtypeStruct((b, t, ML_PAD_W), F32),
                     jax.ShapeDtypeStruct((b, t, ML_PAD_W), F32)) + state_shapes
        out_specs = (pl.BlockSpec((1, L, ML_PAD_W), fwd), pl.BlockSpec((1, L, ML_PAD_W), bwd)) + state_specs
    else:
        out_shape, out_specs = state_shapes, state_specs
    return pl.pallas_call(
        functools.partial(_mlstm_kernel, need_out=need_out),
        out_shape=out_shape,
        grid=(b, nc),
        in_specs=in_specs,
        out_specs=out_specs,
        scratch_shapes=[pltpu.VMEM((g2, HEAD_PAD, HEAD_PAD), F32), pltpu.VMEM((g2, 1, LANE), F32)],
        compiler_params=_cparams("arbitrary", "arbitrary"),
        name="mlstm_out" if need_out else "mlstm_state",
    )(qk, qk, p_ml, p_ml, gc, gc, gr, gr, gbc, gbr, cn_in, m_in)


def _fourier_kernel(ct_ref, st_ref, a_ref, b_ref, o_ref):
    @pl.when(pl.program_id(1) == 0)
    def _():
        o_ref[...] = jnp.zeros_like(o_ref)
    o_ref[...] += _dot(ct_ref[...], a_ref[...]) + _dot(st_ref[...], b_ref[...])


def _fourier_call(ct, mst, fa, fb):
    t, n = fa.shape
    tm = min(t, 512)
    return pl.pallas_call(
        _fourier_kernel,
        out_shape=jax.ShapeDtypeStruct((t, n), F32),
        grid=(t // tm, t // tm),
        in_specs=[pl.BlockSpec((tm, tm), lambda i, k: (i, k)),
                  pl.BlockSpec((tm, tm), lambda i, k: (i, k)),
                  pl.BlockSpec((tm, n), lambda i, k: (k, 0)),
                  pl.BlockSpec((tm, n), lambda i, k: (k, 0))],
        out_specs=pl.BlockSpec((tm, n), lambda i, k: (i, 0)),
        compiler_params=_cparams("arbitrary", "arbitrary"),
        name="fourier_pos",
    )(ct, mst, fa, fb)


def _merge_kernel(x_ref, m2_ref, m3_ref, m4_ref, hf_ref, hb_ref, o_ref, zgm_ref, cv_ref, zft_ref, gate_ref,
                  mlg_ref, cvg_ref, cvb_ref, n2g_ref, wb0_ref, wb1_ref, wb2_ref, wb3_ref, wout_ref,
                  xo_ref, h2_ref):
    hsum = hf_ref[0] + hb_ref[0]
    parts = []
    for hd in range(ML_HEADS):
        blk = hsum[:, hd * HEAD_PAD:(hd + 1) * HEAD_PAD]
        ms = jnp.sum(blk * blk, axis=-1, keepdims=True) * (1.0 / ML_DH)
        parts.append(blk * lax.rsqrt(ms + EPS))
    z_ml = jnp.concatenate(parts, axis=1) * mlg_ref[...] * _sigmoid(o_ref[0])
    z_cv = _silu(_layernorm(cv_ref[0], cvg_ref[...], cvb_ref[...]))

    def gate(i):
        return gate_ref[0, :, i * D_MODEL:(i + 1) * D_MODEL].astype(F32)

    y = gate(0) * _dot(z_ml.astype(BF16), wb0_ref[...])
    y = y + gate(1) * _dot(zgm_ref[0].astype(BF16), wb1_ref[...])
    y = y + gate(2) * _dot(z_cv.astype(BF16), wb2_ref[...])
    y = y + gate(3) * _dot(zft_ref[...].astype(BF16), wb3_ref[...])
    xo = x_ref[0] + m2_ref[0, 0] * _dot(y.astype(BF16), wout_ref[...])
    xo_ref[0] = xo
    h2_ref[0] = _rms_mod(xo, n2g_ref[...], m3_ref[0, 0], m4_ref[0, 0])


def _merge_call(x, mod, row_fn, hf, hb, p_ml, zgm, cv, zft, gate, mlg, cvg, cvb, n2g, wb0, wb1, wb2, wb3, wout):
    b, t, d = x.shape
    tm = 256
    tok = lambda w: pl.BlockSpec((1, tm, w), lambda i, j: (i, j, 0))
    rf = lambda i, j: row_fn(i)
    return pl.pallas_call(
        _merge_kernel,
        out_shape=(jax.ShapeDtypeStruct((b, t, d), F32), jax.ShapeDtypeStruct((b, t, d), F32)),
        grid=(b, t // tm),
        in_specs=[tok(d), _mod_spec(rf, 2), _mod_spec(rf, 3), _mod_spec(rf, 4),
                  tok(ML_PAD_W), tok(ML_PAD_W),
                  pl.BlockSpec((1, tm, ML_PAD_W), lambda i, j: (i, j, 3)),
                  tok(MIX_W), tok(MIX_W),
                  pl.BlockSpec((tm, MIX_W), lambda i, j: (j, i)),
                  tok(4 * d),
                  _const_spec((1, ML_PAD_W)), _const_spec((1, MIX_W)), _const_spec((1, MIX_W)), _const_spec((1, d)),
                  _const_spec(wb0.shape), _const_spec(wb1.shape), _const_spec(wb2.shape), _const_spec(wb3.shape),
                  _const_spec(wout.shape)],
        out_specs=(tok(d), tok(d)),
        compiler_params=_cparams("arbitrary", "arbitrary"),
        name="merge",
    )(x, mod, mod, mod, hf, hb, p_ml, zgm, cv, zft, gate, mlg, cvg, cvb, n2g, wb0, wb1, wb2, wb3, wout)


def _top16_rows(s, ids):
    big = jnp.int32(2 ** 30)
    out_rows = lax.broadcasted_iota(jnp.int32, (PEER_TOPK, LANE), 0)
    vals = jnp.zeros((PEER_TOPK, LANE), F32)
    sel = jnp.zeros((PEER_TOPK, LANE), jnp.int32)
    for i in range(PEER_TOPK):
        m = jnp.max(s, axis=0, keepdims=True)
        pick = jnp.min(jnp.where(s == m, ids, big), axis=0, keepdims=True)
        s = jnp.where(ids == pick, -jnp.inf, s)
        vals = jnp.where(out_rows == i, m, vals)
        sel = jnp.where(out_rows == i, pick, sel)
    return vals, sel


def _peer_stage2(v1, i1, v2, i2):
    big = jnp.int32(2 ** 30)
    sub = lax.broadcasted_iota(jnp.int32, (8, LANE), 0)
    tiles = []
    for a in range(8):
        va = v1[a:a + 1, :]
        ea = i1[a:a + 1, :] * N_KEYS
        for half in range(2 if a == 0 else 1):
            b = sub + 8 * half
            ok = (a + 1) * (b + 1) <= PEER_TOPK
            sc = jnp.where(ok, va + v2[8 * half:8 * half + 8, :], -jnp.inf)
            tiles.append((sc, a * PEER_TOPK + b, ea + i2[8 * half:8 * half + 8, :]))
    tiles.append((v1[8:16, :] + v2[0:1, :], (sub + 8) * PEER_TOPK, i1[8:16, :] * N_KEYS + i2[0:1, :]))
    sc = jnp.concatenate([t[0] for t in tiles], axis=0)
    pos = jnp.concatenate([jnp.broadcast_to(t[1], (8, LANE)) for t in tiles], axis=0)
    eid = jnp.concatenate([t[2] for t in tiles], axis=0)
    out_rows = lax.broadcasted_iota(jnp.int32, (PEER_TOPK, LANE), 0)
    vals = jnp.zeros((PEER_TOPK, LANE), F32)
    sel = jnp.zeros((PEER_TOPK, LANE), jnp.int32)
    for i in range(PEER_TOPK):
        m = jnp.max(sc, axis=0, keepdims=True)
        pick = jnp.min(jnp.where(sc == m, pos, big), axis=0, keepdims=True)
        hit = pos == pick
        e = jnp.max(jnp.where(hit, eid, -1), axis=0, keepdims=True)
        sc = jnp.where(hit, -jnp.inf, sc)
        vals = jnp.where(out_rows == i, m, vals)
        sel = jnp.where(out_rows == i, e, sel)
    return vals, sel


def _route_kernel(h_ref, wq_ref, keys_ref, e_ref, g_ref):
    tm = h_ref.shape[0]
    q = _dot(h_ref[...].astype(BF16), wq_ref[...])
    key_ids = lax.broadcasted_iota(jnp.int32, (N_KEYS, LANE), 0)
    for c in range(tm // LANE):
        ids, ws = [], []
        for hd in range(PEER_HEADS):
            tops = []
            for p in range(2):
                lo = (hd * 2 + p) * N_KEYS
                qs = q[c * LANE:(c + 1) * LANE, lo:lo + N_KEYS].astype(BF16)
                s = _dot_nt(keys_ref[p], qs)
                tops.append(_top16_rows(s, key_ids))
            sc, eid = _peer_stage2(tops[0][0], tops[0][1], tops[1][0], tops[1][1])
            w = jnp.exp(sc - sc[0:1, :])
            ids.append(eid)
            ws.append(w / jnp.sum(w, axis=0, keepdims=True))
        e_ref[c * LANE:(c + 1) * LANE, :] = jnp.concatenate(ids, axis=0).T
        g_ref[c * LANE:(c + 1) * LANE, :] = jnp.concatenate(ws, axis=0).T


def _route_call(h2, wq, keys):
    n, d = h2.shape
    tm = 256
    return pl.pallas_call(
        _route_kernel,
        out_shape=(jax.ShapeDtypeStruct((n, PEER_SEL), jnp.int32), jax.ShapeDtypeStruct((n, PEER_SEL), F32)),
        grid=(n // tm,),
        in_specs=[pl.BlockSpec((tm, d), lambda i: (i, 0)), _const_spec(wq.shape), _const_spec(keys.shape)],
        out_specs=(pl.BlockSpec((tm, PEER_SEL), lambda i: (i, 0)), pl.BlockSpec((tm, PEER_SEL), lambda i: (i, 0))),
        compiler_params=_cparams("arbitrary"),
        name="peer_route",
    )(h2, wq, keys)


PEER_TB = 64
PEER_NBUF = 4
ROW_TILES = 2 * D_MODEL // LANE


def _experts_kernel(idx_ref, g_ref, h_ref, x_ref, m5_ref, fg_ref, uv_ref, fill_ref, o_ref, buf, sem, *, final):
    tb = h_ref.shape[0]
    nt = D_MODEL // LANE

    def gather(t, slot):
        for r in range(PEER_SEL):
            pltpu.make_async_copy(uv_ref.at[idx_ref[t, r]], buf.at[slot, :, pl.ds(r, 1), :], sem.at[slot]).start()

    def wait(slot):
        pltpu.make_async_copy(fill_ref, buf.at[slot], sem.at[slot]).wait()

    eye = (lax.broadcasted_iota(jnp.int32, (PEER_SEL, PEER_SEL), 0)
           == lax.broadcasted_iota(jnp.int32, (PEER_SEL, PEER_SEL), 1))

    def compute(t, slot):
        hrow = h_ref[pl.ds(t, 1), :]
        acc = buf[slot, 0] * hrow[:, 0:LANE]
        for c in range(1, nt):
            acc = acc + buf[slot, c] * hrow[:, c * LANE:(c + 1) * LANE]
        s = jnp.sum(acc, axis=1, keepdims=True)
        gcol = jnp.sum(jnp.where(eye, g_ref[pl.ds(t, 1), :], 0.0), axis=1, keepdims=True)
        a = _gelu(s) * gcol
        o = jnp.concatenate([jnp.sum(buf[slot, nt + c] * a, axis=0, keepdims=True) for c in range(nt)], axis=1)
        xo = x_ref[pl.ds(t, 1), :] + m5_ref[0, 0] * o
        if final:
            xo = xo * lax.rsqrt(jnp.mean(xo * xo, axis=-1, keepdims=True) + EPS) * fg_ref[...]
        o_ref[pl.ds(t, 1), :] = xo

    for slot in range(PEER_NBUF):
        gather(slot, slot)

    def body(i, carry):
        for slot in range(PEER_NBUF):
            t = i * PEER_NBUF + slot
            wait(slot)
            compute(t, slot)

            @pl.when(t + PEER_NBUF < tb)
            def _():
                gather(t + PEER_NBUF, slot)
        return carry

    lax.fori_loop(0, tb // PEER_NBUF, body, 0)


def _experts_call(idx, g, h2, x, mod, row_of_block, final_g, uv, final):
    n, d = h2.shape
    tb = PEER_TB
    tok = lambda w: pl.BlockSpec((tb, w), lambda i: (i, 0))
    fill = jnp.zeros((ROW_TILES, PEER_SEL, LANE), F32)
    return pl.pallas_call(
        functools.partial(_experts_kernel, final=final),
        out_shape=jax.ShapeDtypeStruct((n, d), F32),
        grid=(n // tb,),
        in_specs=[pl.BlockSpec((tb, PEER_SEL), lambda i: (i, 0), memory_space=pltpu.SMEM),
                  tok(PEER_SEL), tok(d), tok(d),
                  _mod_spec(lambda i: row_of_block(i), 5),
                  _const_spec((1, d)),
                  pl.BlockSpec(memory_space=pl.ANY),
                  pl.BlockSpec(memory_space=pl.ANY)],
        out_specs=tok(d),
        scratch_shapes=[pltpu.VMEM((PEER_NBUF, ROW_TILES, PEER_SEL, LANE), F32),
                        pltpu.SemaphoreType.DMA((PEER_NBUF,))],
        compiler_params=_cparams("arbitrary"),
        name="peer_experts_final" if final else "peer_experts",
    )(idx, g, h2, x, mod, final_g, uv, fill)


def _grid_sincos(n_tok, d):
    rows = n_tok // GRID_W
    n_freq = d // 4
    freq = 1.0 / (10000.0 ** (jnp.arange(n_freq, dtype=F32) / n_freq))
    r = jnp.repeat(jnp.arange(rows, dtype=F32), GRID_W)
    cc = jnp.tile(jnp.arange(GRID_W, dtype=F32), rows)
    ar = r[:, None] * freq[None, :]
    ac = cc[:, None] * freq[None, :]
    return jnp.concatenate([jnp.sin(ar), jnp.cos(ar), jnp.sin(ac), jnp.cos(ac)], axis=-1)


def _dft_tables(t_len):
    i = jnp.arange(t_len, dtype=jnp.int32)
    ph = (i[:, None] * i[None, :]) % t_len
    ang = ph.astype(F32) * (2.0 * math.pi / t_len)
    scale = 1.0 / math.sqrt(t_len * FT_DG)
    return (jnp.cos(ang) * scale).astype(BF16), (-jnp.sin(ang) * scale).astype(BF16)


def _channel_dft():
    i = jnp.arange(MIX_W, dtype=jnp.int32)
    same = (i[:, None] // FT_DG) == (i[None, :] // FT_DG)
    ph = ((i[:, None] % FT_DG) * (i[None, :] % FT_DG)) % FT_DG
    ang = ph.astype(F32) * (2.0 * math.pi / FT_DG)
    c = jnp.where(same, jnp.cos(ang), 0.0)
    s = jnp.where(same, jnp.sin(ang), 0.0)
    return jnp.concatenate([c, s], axis=1).astype(BF16)


def _pad_heads(a):
    lead = a.shape[:-1]
    a = a.reshape(lead + (ML_HEADS, ML_DH))
    a = jnp.pad(a, [(0, 0)] * len(lead) + [(0, 0), (0, HEAD_PAD - ML_DH)])
    return a.reshape(lead + (ML_PAD_W,))


def _layer_weights(l, w_in, ml_conv_w, ml_conv_b, ml_gate_b, ml_norm_g, gm_w_s, gm_b_s, w_branch):
    w = w_in[l]
    q, k, v = w[:, 0:256], w[:, 256:512], w[:, 512:768]
    gates, o = w[:, 768:784], w[:, 784:1040]
    w_ml = jnp.concatenate([_pad_heads(q), _pad_heads(k), _pad_heads(v), _pad_heads(o)], axis=1).astype(BF16)
    w_gc = jnp.pad(gates, ((0, 0), (0, LANE - 16))).astype(BF16)
    w_gr = gates.T.astype(BF16)
    w_rest = w[:, 1040:].astype(BF16)
    cw = ml_conv_w[l]
    conv_w = jnp.concatenate([_pad_heads(cw[:, :256]), _pad_heads(cw[:, 256:])], axis=1)
    cb = ml_conv_b[l]
    conv_b = jnp.concatenate([_pad_heads(cb[:256]), _pad_heads(cb[256:])])[None, :]
    gb = ml_gate_b[l]
    gbc = jnp.pad(gb, (0, LANE - 16))[None, :]
    gbr = jnp.broadcast_to(gb[:, None], (16, ML_CHUNK))
    mlg = _pad_heads(ml_norm_g[l].reshape(-1))[None, :]
    wcat = jnp.transpose(gm_w_s[l], (1, 0, 2)).reshape(GM_CHUNK, GM_GROUPS * GM_CHUNK).astype(BF16)
    bias_gm = jnp.repeat(gm_b_s[l].T, GM_DG, axis=1)
    wb = w_branch[l]
    wb0 = jnp.pad(wb[0].reshape(ML_HEADS, ML_DH, D_MODEL), ((0, 0), (0, HEAD_PAD - ML_DH), (0, 0)))
    wb0 = wb0.reshape(ML_PAD_W, D_MODEL).astype(BF16)
    return dict(w_ml=w_ml, w_gc=w_gc, w_gr=w_gr, w_rest=w_rest, conv_w=conv_w, conv_b=conv_b, gbc=gbc, gbr=gbr,
                mlg=mlg, wcat=wcat, bias_gm=bias_gm, wb0=wb0,
                wb1=wb[1].astype(BF16), wb2=wb[2].astype(BF16), wb3=wb[3].astype(BF16))


def kernel(x, c, ctx, c_ctx, w_ada, b_ada, norm1_g, norm2_g, w_in, ml_conv_w, ml_conv_b, ml_gate_b, ml_norm_g,
           gm_ln_g, gm_ln_b, gm_w_s, gm_b_s, cv_dw_w, cv_dw_b, cv_ln_g, cv_ln_b, w_branch, w_out, peer_w_q,
           peer_keys, peer_u, peer_v, final_norm_g):
    b_sz, n_tok, d = x.shape
    n_ctx = ctx.shape[1]
    depth = w_ada.shape[0]
    ctx_row = b_sz

    cond = jnp.zeros((16, d), F32).at[:b_sz].set(c).at[ctx_row].set(c_ctx)
    mod_all = _ada_call(cond, w_ada, b_ada)
    x = _addpos_call(x, _grid_sincos(n_tok, d))
    xc = ctx
    cs = _channel_dft()
    dft_x = _dft_tables(n_tok)
    dft_c = _dft_tables(n_ctx)
    final_g = final_norm_g[None, :]
    lat_row = lambda i: i
    ctx_rowf = lambda i: ctx_row

    def mixers(xs, l, lw, mod, row_fn, dft, cn, m, need_out):
        p_ml, gc, gr = _inproj_ml_call(xs, mod, row_fn, norm1_g[l][None], lw["w_ml"], lw["w_gc"], lw["w_gr"])
        qk = _dwconv_call(p_ml, 0, 2 * ML_PAD_W, lw["conv_w"], lw["conv_b"], True)
        res = _mlstm_call(qk, p_ml, gc, gr, lw["gbc"], lw["gbr"], cn, m, need_out)
        if not need_out:
            return None, res[0], res[1]
        hf, hb, cn, m = res
        zgm, zglu, fa, fb, gate = _inproj_rest_call(xs, mod, row_fn, norm1_g[l][None], lw["w_rest"],
                                                    gm_ln_g[l][None], gm_ln_b[l][None], lw["wcat"],
                                                    lw["bias_gm"], cs)
        cv = _dwconv_call(zglu, 0, MIX_W, cv_dw_w[l], cv_dw_b[l][None], False)
        zft = _fourier_call(dft[0], dft[1], fa, fb)
        xo, h2 = _merge_call(xs, mod, row_fn, hf, hb, p_ml, zgm, cv, zft, gate, lw["mlg"], cv_ln_g[l][None],
                             cv_ln_b[l][None], norm2_g[l][None], lw["wb0"], lw["wb1"], lw["wb2"], lw["wb3"],
                             w_out[l].astype(BF16))
        return (xo, h2), cn, m

    def peer(xo, h2, l, mod, row_of_block, uv, final):
        bsz, t, _ = xo.shape
        n = bsz * t
        h2f = h2.reshape(n, d)
        e, g = _route_call(h2f, peer_w_q[l].astype(BF16), peer_keys[l].astype(BF16))
        out = _experts_call(e, g, h2f, xo.reshape(n, d), mod, row_of_block, final_g, uv, final)
        return out.reshape(bsz, t, d)

    for l in range(depth):
        last = l == depth - 1
        lw = _layer_weights(l, w_in, ml_conv_w, ml_conv_b, ml_gate_b, ml_norm_g, gm_w_s, gm_b_s, w_branch)
        mod = mod_all[l].reshape(16, 6, 1, d)
        uv = jnp.concatenate([peer_u[l], peer_v[l]], axis=1).reshape(-1, ROW_TILES, 1, LANE)
        cn0 = jnp.zeros((b_sz, 2 * ML_HEADS, HEAD_PAD, HEAD_PAD), F32)
        m0 = jnp.zeros((b_sz, 2 * ML_HEADS, 1, LANE), F32)
        res_c, cn, m = mixers(xc, l, lw, mod, ctx_rowf, dft_c, cn0, m0, not last)
        res_x, _, _ = mixers(x, l, lw, mod, lat_row, dft_x, cn, m, True)
        blocks_per_batch = n_tok // PEER_TB
        x = peer(res_x[0], res_x[1], l, mod, lambda i: i // blocks_per_batch, uv, last)
        if not last:
            xc = peer(res_c[0], res_c[1], l, mod, lambda i: ctx_row, uv, False)
    return x
```

```python
import functools
import math

import jax
import jax.numpy as jnp
from jax import lax
from jax.experimental import pallas as pl
from jax.experimental.pallas import tpu as pltpu
from jax.experimental.pallas import tpu_sc as plsc

F32 = jnp.float32
BF16 = jnp.bfloat16
HIGHEST = lax.Precision.HIGHEST

D_MODEL = 1024
GRID_W = 64
EPS = 1e-6
MIX_W = 256
ML_HEADS = 4
ML_DH = 64
ML_CHUNK = 128
GM_CHUNK = 128
GM_GROUPS = 4
GM_DG = 64
CV_K = 31
ML_CONV = 3
FT_GROUPS = 4
FT_DG = 64
PEER_HEADS = 8
PEER_TOPK = 16
N_KEYS = 128
PEER_SEL = PEER_HEADS * PEER_TOPK

LANE = 128
HEAD_PAD = LANE
ML_PAD_W = ML_HEADS * HEAD_PAD
CONV_HALO = 16
VMEM_LIMIT = 56 * 1024 * 1024


def _cparams(*sem):
    return pltpu.CompilerParams(dimension_semantics=sem, vmem_limit_bytes=VMEM_LIMIT)


def _sigmoid(x):
    return 1.0 / (1.0 + jnp.exp(-x))


def _silu(x):
    return x * _sigmoid(x)


def _gelu(x):
    return 0.5 * x * (1.0 + jnp.tanh(math.sqrt(2.0 / math.pi) * (x + 0.044715 * (x * x * x))))


def _log_sigmoid(x):
    return jnp.minimum(x, 0.0) - jnp.log(1.0 + jnp.exp(-jnp.abs(x)))


def _rms_mod(x, g, shift, scale):
    y = x * lax.rsqrt(jnp.mean(x * x, axis=-1, keepdims=True) + EPS) * g
    return y * (1.0 + scale) + shift


def _layernorm(x, g, b):
    mu = jnp.mean(x, axis=-1, keepdims=True)
    xc = x - mu
    return xc * lax.rsqrt(jnp.mean(xc * xc, axis=-1, keepdims=True) + EPS) * g + b


def _dot(a, b):
    return jnp.dot(a, b, preferred_element_type=F32)


def _dot_nt(a, b):
    return lax.dot_general(a, b, (((1,), (1,)), ((), ())), preferred_element_type=F32)


def _dot_exact(a, b):
    return jnp.dot(a, b, preferred_element_type=F32, precision=HIGHEST)


def _ada_kernel(s_ref, w_ref, b_ref, o_ref):
    s = _silu(s_ref[...])
    o_ref[0] = _dot_exact(s, w_ref[0]) + b_ref[0]


def _ada_call(cond, w_ada, b_ada):
    depth, d, n6 = w_ada.shape
    rows = cond.shape[0]
    tn = 1536
    return pl.pallas_call(
        _ada_kernel,
        out_shape=jax.ShapeDtypeStruct((depth, rows, n6), F32),
        grid=(depth, n6 // tn),
        in_specs=[pl.BlockSpec((rows, d), lambda l, j: (0, 0)),
                  pl.BlockSpec((1, d, tn), lambda l, j: (l, 0, j)),
                  pl.BlockSpec((1, 1, tn), lambda l, j: (l, 0, j))],
        out_specs=pl.BlockSpec((1, rows, tn), lambda l, j: (l, 0, j)),
        compiler_params=_cparams("arbitrary", "arbitrary"),
        name="ada_mod",
    )(cond, w_ada, b_ada.reshape(depth, 1, n6))


def _addpos_kernel(x_ref, p_ref, o_ref):
    o_ref[0] = x_ref[0] + p_ref[...]


def _addpos_call(x, pos):
    b, t, d = x.shape
    tm = min(t, 512)
    return pl.pallas_call(
        _addpos_kernel,
        out_shape=jax.ShapeDtypeStruct(x.shape, F32),
        grid=(t // tm, b),
        in_specs=[pl.BlockSpec((1, tm, d), lambda i, j: (j, i, 0)),
                  pl.BlockSpec((tm, d), lambda i, j: (i, 0))],
        out_specs=pl.BlockSpec((1, tm, d), lambda i, j: (j, i, 0)),
        compiler_params=_cparams("arbitrary", "arbitrary"),
        name="add_pos",
    )(x, pos)


def _mod_spec(row_fn, k):
    return pl.BlockSpec((1, 1, 1, D_MODEL), lambda *g: (row_fn(*g), k, 0, 0))


def _const_spec(shape):
    nd = len(shape)
    return pl.BlockSpec(shape, lambda *g: (0,) * nd)


def _inproj_ml_kernel(x_ref, sh_ref, sc_ref, g_ref, w_ref, wgc_ref, wgr_ref, p_ref, gc_ref, gr_ref):
    h = _rms_mod(x_ref[0], g_ref[...], sh_ref[0, 0], sc_ref[0, 0]).astype(BF16)
    p_ref[0] = _dot(h, w_ref[...])
    gc_ref[0] = _dot(h, wgc_ref[...])
    gr_ref[0] = _dot_nt(wgr_ref[...], h)


def _inproj_ml_call(x, mod, row_fn, norm_g, w_ml, w_gc, w_gr):
    b, t, d = x.shape
    tm = 256
    n_ml = w_ml.shape[1]
    return pl.pallas_call(
        _inproj_ml_kernel,
        out_shape=(jax.ShapeDtypeStruct((b, t, n_ml), F32),
                   jax.ShapeDtypeStruct((b, t, LANE), F32),
                   jax.ShapeDtypeStruct((b, 16, t), F32)),
        grid=(b, t // tm),
        in_specs=[pl.BlockSpec((1, tm, d), lambda i, j: (i, j, 0)),
                  _mod_spec(lambda i, j: row_fn(i), 0),
                  _mod_spec(lambda i, j: row_fn(i), 1),
                  _const_spec((1, d)),
                  _const_spec(w_ml.shape),
                  _const_spec(w_gc.shape),
                  _const_spec(w_gr.shape)],
        out_specs=(pl.BlockSpec((1, tm, n_ml), lambda i, j: (i, j, 0)),
                   pl.BlockSpec((1, tm, LANE), lambda i, j: (i, j, 0)),
                   pl.BlockSpec((1, 16, tm), lambda i, j: (i, 0, j))),
        compiler_params=_cparams("arbitrary", "arbitrary"),
        name="inproj_ml",
    )(x, mod, mod, norm_g, w_ml, w_gc, w_gr)


def _inproj_rest_kernel(x_ref, sh_ref, sc_ref, g_ref, w_ref, lng_ref, lnb_ref, wcat_ref, bgm_ref, cs_ref,
                        zgm_ref, zglu_ref, fa_ref, fb_ref, gate_ref):
    tm = x_ref.shape[1]
    h = _rms_mod(x_ref[0], g_ref[...], sh_ref[0, 0], sc_ref[0, 0]).astype(BF16)
    z = _gelu(_dot(h, w_ref[:, 0:2 * MIX_W]))
    u = z[:, :MIX_W]
    v = _layernorm(z[:, MIX_W:], lng_ref[...], lnb_ref[...])
    grp = lax.broadcasted_iota(jnp.int32, (GM_CHUNK, MIX_W), 1) // GM_DG
    for j in range(tm // GM_CHUNK):
        vc = v[j * GM_CHUNK:(j + 1) * GM_CHUNK]
        vstack = jnp.concatenate([jnp.where(grp == g, vc, 0.0) for g in range(GM_GROUPS)], axis=0).astype(BF16)
        s = _dot(wcat_ref[...], vstack) + bgm_ref[...]
        zgm_ref[0, j * GM_CHUNK:(j + 1) * GM_CHUNK, :] = u[j * GM_CHUNK:(j + 1) * GM_CHUNK] * s
    pc = _dot(h, w_ref[:, 2 * MIX_W:4 * MIX_W])
    zglu_ref[0] = pc[:, :MIX_W] * _sigmoid(pc[:, MIX_W:])
    pf = _dot(h, w_ref[:, 4 * MIX_W:5 * MIX_W]).astype(BF16)
    ab = _dot(pf, cs_ref[...])
    fa_ref[...] = ab[:, :MIX_W].astype(BF16)
    fb_ref[...] = ab[:, MIX_W:].astype(BF16)
    for i in range(4):
        lo = 5 * MIX_W + i * D_MODEL
        gate_ref[0, :, i * D_MODEL:(i + 1) * D_MODEL] = _sigmoid(_dot(h, w_ref[:, lo:lo + D_MODEL])).astype(BF16)


def _inproj_rest_call(x, mod, row_fn, norm_g, w_rest, gm_ln_g, gm_ln_b, wcat, bias_gm, cs):
    b, t, d = x.shape
    tm = 256
    return pl.pallas_call(
        _inproj_rest_kernel,
        out_shape=(jax.ShapeDtypeStruct((b, t, MIX_W), F32),
                   jax.ShapeDtypeStruct((b, t, MIX_W), F32),
                   jax.ShapeDtypeStruct((t, b * MIX_W), BF16),
                   jax.ShapeDtypeStruct((t, b * MIX_W), BF16),
                   jax.ShapeDtypeStruct((b, t, 4 * D_MODEL), BF16)),
        grid=(b, t // tm),
        in_specs=[pl.BlockSpec((1, tm, d), lambda i, j: (i, j, 0)),
                  _mod_spec(lambda i, j: row_fn(i), 0),
                  _mod_spec(lambda i, j: row_fn(i), 1),
                  _const_spec((1, d)),
                  _const_spec(w_rest.shape),
                  _const_spec((1, MIX_W)), _const_spec((1, MIX_W)),
                  _const_spec(wcat.shape), _const_spec(bias_gm.shape), _const_spec(cs.shape)],
        out_specs=(pl.BlockSpec((1, tm, MIX_W), lambda i, j: (i, j, 0)),
                   pl.BlockSpec((1, tm, MIX_W), lambda i, j: (i, j, 0)),
                   pl.BlockSpec((tm, MIX_W), lambda i, j: (j, i)),
                   pl.BlockSpec((tm, MIX_W), lambda i, j: (j, i)),
                   pl.BlockSpec((1, tm, 4 * D_MODEL), lambda i, j: (i, j, 0))),
        compiler_params=_cparams("arbitrary", "arbitrary"),
        name="inproj_rest",
    )(x, mod, mod, norm_g, w_rest, gm_ln_g, gm_ln_b, wcat, bias_gm, cs)


def _dwconv_kernel(x_ref, prev_ref, next_ref, w_ref, b_ref, o_ref, win_ref, *, taps, act):
    i = pl.program_id(2)
    last = pl.num_programs(2) - 1
    rc = x_ref.shape[1]
    win_ref[0:CONV_HALO, :] = jnp.where(i > 0, prev_ref[0], 0.0)
    win_ref[CONV_HALO:CONV_HALO + rc, :] = x_ref[0]
    win_ref[CONV_HALO + rc:, :] = jnp.where(i < last, next_ref[0], 0.0)
    sub = 128
    for r in range(rc // sub):
        acc = jnp.broadcast_to(b_ref[...], (sub, LANE))
        for j in range(taps):
            lo = CONV_HALO + r * sub + j - taps // 2
            acc = acc + w_ref[j:j + 1, :] * win_ref[lo:lo + sub, :]
        if act:
            acc = _silu(acc)
        o_ref[0, r * sub:(r + 1) * sub, :] = acc


def _dwconv_call(x, col0, ncol, w, bias, act):
    b, t, _ = x.shape
    taps = w.shape[0]
    rc = min(t, 512)
    c0 = col0 // LANE
    hb = rc // CONV_HALO
    nh = t // CONV_HALO
    return pl.pallas_call(
        functools.partial(_dwconv_kernel, taps=taps, act=act),
        out_shape=jax.ShapeDtypeStruct((b, t, ncol), F32),
        grid=(b, ncol // LANE, t // rc),
        in_specs=[pl.BlockSpec((1, rc, LANE), lambda bi, c, i: (bi, i, c0 + c)),
                  pl.BlockSpec((1, CONV_HALO, LANE), lambda bi, c, i: (bi, jnp.maximum(i * hb - 1, 0), c0 + c)),
                  pl.BlockSpec((1, CONV_HALO, LANE), lambda bi, c, i: (bi, jnp.minimum((i + 1) * hb, nh - 1), c0 + c)),
                  pl.BlockSpec((taps, LANE), lambda bi, c, i: (0, c)),
                  pl.BlockSpec((1, LANE), lambda bi, c, i: (0, c))],
        out_specs=pl.BlockSpec((1, rc, LANE), lambda bi, c, i: (bi, i, c)),
        scratch_shapes=[pltpu.VMEM((rc + 2 * CONV_HALO, LANE), F32)],
        compiler_params=_cparams("arbitrary", "arbitrary", "arbitrary"),
        name="dwconv_k%d" % taps,
    )(x, x, x, w, bias)


def _mlstm_kernel(*refs, need_out):
    (qkf_ref, qkb_ref, vf_ref, vb_ref, gcf_ref, gcb_ref, grf_ref, grb_ref, gbc_ref, gbr_ref,
     cn_in_ref, m_in_ref) = refs[:12]
    if need_out:
        hf_ref, hb_ref, cn_out_ref, m_out_ref, cn_s, m_s = refs[12:]
    else:
        cn_out_ref, m_out_ref, cn_s, m_s = refs[12:]
        hf_ref = hb_ref = None
    j = pl.program_id(1)
    last = pl.num_programs(1) - 1
    L = ML_CHUNK

    @pl.when(j == 0)
    def _():
        cn_s[...] = cn_in_ref[0]
        m_s[...] = m_in_ref[0]

    row = lax.broadcasted_iota(jnp.int32, (L, L), 0)
    col = lax.broadcasted_iota(jnp.int32, (L, L), 1)
    lane = lax.broadcasted_iota(jnp.int32, (1, HEAD_PAD), 1)
    one_at_dh = (lane == ML_DH).astype(F32)
    keep_dh = (lane < ML_DH).astype(F32)

    for direction in range(2):
        qk_ref, v_ref, gc_ref, gr_ref, h_ref = ((qkf_ref, vf_ref, gcf_ref, grf_ref, hf_ref) if direction == 0
                                                else (qkb_ref, vb_ref, gcb_ref, grb_ref, hb_ref))
        visible = (col <= row) if direction == 0 else (col >= row)
        tri_c = visible.astype(F32)
        tri_r = ((row <= col) if direction == 0 else (row >= col)).astype(F32)
        gcv = gc_ref[0] + gbc_ref[...]
        grv = gr_ref[0] + gbr_ref[...]
        lf_c = _log_sigmoid(gcv)
        lf_r = _log_sigmoid(grv)
        cum_c = _dot_exact(tri_c, lf_c)
        cum_r = _dot_exact(lf_r, tri_r)
        tot_c = jnp.sum(lf_c, axis=0, keepdims=True)
        for hd in range(ML_HEADS):
            g = direction * ML_HEADS + hd
            gi_col = g
            gf_col = 2 * ML_HEADS + g
            li_r = grv[gi_col:gi_col + 1, :]
            li_c = gcv[:, gi_col:gi_col + 1]
            cc = cum_c[:, gf_col:gf_col + 1]
            cr = cum_r[gf_col:gf_col + 1, :]
            tot = tot_c[:, gf_col:gf_col + 1]
            m_prev = m_s[g][:, 0:1]
            cn = cn_s[g]
            qh = qk_ref[0, :, hd * HEAD_PAD:(hd + 1) * HEAD_PAD]
            kh = qk_ref[0, :, ML_PAD_W + hd * HEAD_PAD:ML_PAD_W + (hd + 1) * HEAD_PAD] * (ML_DH ** -0.5)
            v1 = v_ref[0, :, hd * HEAD_PAD:(hd + 1) * HEAD_PAD] + one_at_dh
            w_r = tot - cr + li_r
            w_c = tot - cc + li_c
            m_new = jnp.maximum(tot + m_prev, jnp.max(w_r, axis=1, keepdims=True))
            ws_c = jnp.exp(w_c - m_new)
            decay = jnp.exp(tot + m_prev - m_new)
            if need_out:
                dlog = jnp.where(visible, cc - cr + li_r, -jnp.inf)
                inter = cc + m_prev
                m_t = jnp.maximum(inter, jnp.max(dlog, axis=1, keepdims=True))
                s = _dot_nt(qh.astype(BF16), kh.astype(BF16)) * jnp.exp(dlog - m_t)
                gi = jnp.exp(inter - m_t)
                nd = gi * _dot(qh.astype(BF16), cn.astype(BF16)) + _dot(s.astype(BF16), v1.astype(BF16))
                den = nd[:, ML_DH:ML_DH + 1]
                hh = nd / jnp.maximum(jnp.abs(den), jnp.exp(-m_t))
                h_ref[0, :, hd * HEAD_PAD:(hd + 1) * HEAD_PAD] = hh * keep_dh
            kw = (kh * ws_c).astype(BF16)
            upd = lax.dot_general(kw, v1.astype(BF16), (((0,), (0,)), ((), ())), preferred_element_type=F32)
            cn_s[g] = decay * cn + upd
            m_s[g] = jnp.broadcast_to(m_new, (1, LANE))

    @pl.when(j == last)
    def _():
        cn_out_ref[0] = cn_s[...]
        m_out_ref[0] = m_s[...]


def _mlstm_call(qk, p_ml, gc, gr, gbc, gbr, cn_in, m_in, need_out):
    b, t, _ = qk.shape
    L = ML_CHUNK
    nc = t // L
    g2 = 2 * ML_HEADS
    fwd = lambda i, j: (i, j, 0)
    bwd = lambda i, j: (i, nc - 1 - j, 0)
    in_specs = [pl.BlockSpec((1, L, 2 * ML_PAD_W), fwd),
                pl.BlockSpec((1, L, 2 * ML_PAD_W), bwd),
                pl.BlockSpec((1, L, ML_PAD_W), lambda i, j: (i, j, 2)),
                pl.BlockSpec((1, L, ML_PAD_W), lambda i, j: (i, nc - 1 - j, 2)),
                pl.BlockSpec((1, L, LANE), fwd),
                pl.BlockSpec((1, L, LANE), bwd),
                pl.BlockSpec((1, 16, L), lambda i, j: (i, 0, j)),
                pl.BlockSpec((1, 16, L), lambda i, j: (i, 0, nc - 1 - j)),
                _const_spec((1, LANE)),
                _const_spec((16, L)),
                pl.BlockSpec((1, g2, HEAD_PAD, HEAD_PAD), lambda i, j: (i, 0, 0, 0)),
                pl.BlockSpec((1, g2, 1, LANE), lambda i, j: (i, 0, 0, 0))]
    state_shapes = (jax.ShapeDtypeStruct((b, g2, HEAD_PAD, HEAD_PAD), F32),
                    jax.ShapeDtypeStruct((b, g2, 1, LANE), F32))
    state_specs = (pl.BlockSpec((1, g2, HEAD_PAD, HEAD_PAD), lambda i, j: (i, 0, 0, 0)),
                   pl.BlockSpec((1, g2, 1, LANE), lambda i, j: (i, 0, 0, 0)))
    if need_out:
        out_shape = (jax.ShapeDtypeStruct((b, t, ML_PAD_W), F32),
                     jax.ShapeDtypeStruct((b, t, ML_PAD_W), F32)) + state_shapes
        out_specs = (pl.BlockSpec((1, L, ML_PAD_W), fwd), pl.BlockSpec((1, L, ML_PAD_W), bwd)) + state_specs
    else:
        out_shape, out_specs = state_shapes, state_specs
    return pl.pallas_call(
        functools.partial(_mlstm_kernel, need_out=need_out),
        out_shape=out_shape,
        grid=(b, nc),
        in_specs=in_specs,
        out_specs=out_specs,
        scratch_shapes=[pltpu.VMEM((g2, HEAD_PAD, HEAD_PAD), F32), pltpu.VMEM((g2, 1, LANE), F32)],
        compiler_params=_cparams("arbitrary", "arbitrary"),
        name="mlstm_out" if need_out else "mlstm_state",
    )(qk, qk, p_ml, p_ml, gc, gc, gr, gr, gbc, gbr, cn_in, m_in)


def _fourier_kernel(ct_ref, st_ref, a_ref, b_ref, o_ref):
    @pl.when(pl.program_id(1) == 0)
    def _():
        o_ref[...] = jnp.zeros_like(o_ref)
    o_ref[...] += _dot(ct_ref[...], a_ref[...]) + _dot(st_ref[...], b_ref[...])


def _fourier_call(ct, mst, fa, fb):
    t, n = fa.shape
    tm = min(t, 512)
    return pl.pallas_call(
        _fourier_kernel,
        out_shape=jax.ShapeDtypeStruct((t, n), F32),
        grid=(t // tm, t // tm),
        in_specs=[pl.BlockSpec((tm, tm), lambda i, k: (i, k)),
                  pl.BlockSpec((tm, tm), lambda i, k: (i, k)),
                  pl.BlockSpec((tm, n), lambda i, k: (k, 0)),
                  pl.BlockSpec((tm, n), lambda i, k: (k, 0))],
        out_specs=pl.BlockSpec((tm, n), lambda i, k: (i, 0)),
        compiler_params=_cparams("arbitrary", "arbitrary"),
        name="fourier_pos",
    )(ct, mst, fa, fb)


def _merge_kernel(x_ref, m2_ref, m3_ref, m4_ref, hf_ref, hb_ref, o_ref, zgm_ref, cv_ref, zft_ref, gate_ref,
                  mlg_ref, cvg_ref, cvb_ref, n2g_ref, wb0_ref, wb1_ref, wb2_ref, wb3_ref, wout_ref,
                  xo_ref, h2_ref):
    hsum = hf_ref[0] + hb_ref[0]
    parts = []
    for hd in range(ML_HEADS):
        blk = hsum[:, hd * HEAD_PAD:(hd + 1) * HEAD_PAD]
        ms = jnp.sum(blk * blk, axis=-1, keepdims=True) * (1.0 / ML_DH)
        parts.append(blk * lax.rsqrt(ms + EPS))
    z_ml = jnp.concatenate(parts, axis=1) * mlg_ref[...] * _sigmoid(o_ref[0])
    z_cv = _silu(_layernorm(cv_ref[0], cvg_ref[...], cvb_ref[...]))

    def gate(i):
        return gate_ref[0, :, i * D_MODEL:(i + 1) * D_MODEL].astype(F32)

    y = gate(0) * _dot(z_ml.astype(BF16), wb0_ref[...])
    y = y + gate(1) * _dot(zgm_ref[0].astype(BF16), wb1_ref[...])
    y = y + gate(2) * _dot(z_cv.astype(BF16), wb2_ref[...])
    y = y + gate(3) * _dot(zft_ref[...].astype(BF16), wb3_ref[...])
    xo = x_ref[0] + m2_ref[0, 0] * _dot(y.astype(BF16), wout_ref[...])
    xo_ref[0] = xo
    h2_ref[0] = _rms_mod(xo, n2g_ref[...], m3_ref[0, 0], m4_ref[0, 0])


def _merge_call(x, mod, row_fn, hf, hb, p_ml, zgm, cv, zft, gate, mlg, cvg, cvb, n2g, wb0, wb1, wb2, wb3, wout):
    b, t, d = x.shape
    tm = 256
    tok = lambda w: pl.BlockSpec((1, tm, w), lambda i, j: (i, j, 0))
    rf = lambda i, j: row_fn(i)
    return pl.pallas_call(
        _merge_kernel,
        out_shape=(jax.ShapeDtypeStruct((b, t, d), F32), jax.ShapeDtypeStruct((b, t, d), F32)),
        grid=(b, t // tm),
        in_specs=[tok(d), _mod_spec(rf, 2), _mod_spec(rf, 3), _mod_spec(rf, 4),
                  tok(ML_PAD_W), tok(ML_PAD_W),
                  pl.BlockSpec((1, tm, ML_PAD_W), lambda i, j: (i, j, 3)),
                  tok(MIX_W), tok(MIX_W),
                  pl.BlockSpec((tm, MIX_W), lambda i, j: (j, i)),
                  tok(4 * d),
                  _const_spec((1, ML_PAD_W)), _const_spec((1, MIX_W)), _const_spec((1, MIX_W)), _const_spec((1, d)),
                  _const_spec(wb0.shape), _const_spec(wb1.shape), _const_spec(wb2.shape), _const_spec(wb3.shape),
                  _const_spec(wout.shape)],
        out_specs=(tok(d), tok(d)),
        compiler_params=_cparams("arbitrary", "arbitrary"),
        name="merge",
    )(x, mod, mod, mod, hf, hb, p_ml, zgm, cv, zft, gate, mlg, cvg, cvb, n2g, wb0, wb1, wb2, wb3, wout)


def _top16_rows(s, ids):
    big = jnp.int32(2 ** 30)
    out_rows = lax.broadcasted_iota(jnp.int32, (PEER_TOPK, LANE), 0)
    vals = jnp.zeros((PEER_TOPK, LANE), F32)
    sel = jnp.zeros((PEER_TOPK, LANE), jnp.int32)
    for i in range(PEER_TOPK):
        m = jnp.max(s, axis=0, keepdims=True)
        pick = jnp.min(jnp.where(s == m, ids, big), axis=0, keepdims=True)
        s = jnp.where(ids == pick, -jnp.inf, s)
        vals = jnp.where(out_rows == i, m, vals)
        sel = jnp.where(out_rows == i, pick, sel)
    return vals, sel


def _peer_stage2(v1, i1, v2, i2):
    big = jnp.int32(2 ** 30)
    sub = lax.broadcasted_iota(jnp.int32, (8, LANE), 0)
    tiles = []
    for a in range(8):
        va = v1[a:a + 1, :]
        ea = i1[a:a + 1, :] * N_KEYS
        for half in range(2 if a == 0 else 1):
            b = sub + 8 * half
            ok = (a + 1) * (b + 1) <= PEER_TOPK
            sc = jnp.where(ok, va + v2[8 * half:8 * half + 8, :], -jnp.inf)
            tiles.append((sc, a * PEER_TOPK + b, ea + i2[8 * half:8 * half + 8, :]))
    tiles.append((v1[8:16, :] + v2[0:1, :], (sub + 8) * PEER_TOPK, i1[8:16, :] * N_KEYS + i2[0:1, :]))
    sc = jnp.concatenate([t[0] for t in tiles], axis=0)
    pos = jnp.concatenate([jnp.broadcast_to(t[1], (8, LANE)) for t in tiles], axis=0)
    eid = jnp.concatenate([t[2] for t in tiles], axis=0)
    out_rows = lax.broadcasted_iota(jnp.int32, (PEER_TOPK, LANE), 0)
    vals = jnp.zeros((PEER_TOPK, LANE), F32)
    sel = jnp.zeros((PEER_TOPK, LANE), jnp.int32)
    for i in range(PEER_TOPK):
        m = jnp.max(sc, axis=0, keepdims=True)
        pick = jnp.min(jnp.where(sc == m, pos, big), axis=0, keepdims=True)
        hit = pos == pick
        e = jnp.max(jnp.where(hit, eid, -1), axis=0, keepdims=True)
        sc = jnp.where(hit, -jnp.inf, sc)
        vals = jnp.where(out_rows == i, m, vals)
        sel = jnp.where(out_rows == i, e, sel)
    return vals, sel


def _route_kernel(h_ref, wq_ref, keys_ref, e_ref, g_ref):
    tm = h_ref.shape[0]
    q = _dot(h_ref[...].astype(BF16), wq_ref[...])
    key_ids = lax.broadcasted_iota(jnp.int32, (N_KEYS, LANE), 0)
    for c in range(tm // LANE):
        ids, ws = [], []
        for hd in range(PEER_HEADS):
            tops = []
            for p in range(2):
                lo = (hd * 2 + p) * N_KEYS
                qs = q[c * LANE:(c + 1) * LANE, lo:lo + N_KEYS].astype(BF16)
                s = _dot_nt(keys_ref[p], qs)
                tops.append(_top16_rows(s, key_ids))
            sc, eid = _peer_stage2(tops[0][0], tops[0][1], tops[1][0], tops[1][1])
            w = jnp.exp(sc - sc[0:1, :])
            ids.append(eid)
            ws.append(w / jnp.sum(w, axis=0, keepdims=True))
        e_ref[c * LANE:(c + 1) * LANE, :] = jnp.concatenate(ids, axis=0).T
        g_ref[c * LANE:(c + 1) * LANE, :] = jnp.concatenate(ws, axis=0).T


def _route_call(h2, wq, keys):
    n, d = h2.shape
    tm = 256
    return pl.pallas_call(
        _route_kernel,
        out_shape=(jax.ShapeDtypeStruct((n, PEER_SEL), jnp.int32), jax.ShapeDtypeStruct((n, PEER_SEL), F32)),
        grid=(n // tm,),
        in_specs=[pl.BlockSpec((tm, d), lambda i: (i, 0)), _const_spec(wq.shape), _const_spec(keys.shape)],
        out_specs=(pl.BlockSpec((tm, PEER_SEL), lambda i: (i, 0)), pl.BlockSpec((tm, PEER_SEL), lambda i: (i, 0))),
        compiler_params=_cparams("arbitrary"),
        name="peer_route",
    )(h2, wq, keys)


PEER_TB = 64
PEER_NBUF = 4
ROW_TILES = 2 * D_MODEL // LANE


def _experts_kernel(idx_ref, g_ref, h_ref, x_ref, m5_ref, fg_ref, uv_ref, fill_ref, o_ref, buf, sem, *, final):
    tb = h_ref.shape[0]
    nt = D_MODEL // LANE

    def gather(t, slot):
        for r in range(PEER_SEL):
            pltpu.make_async_copy(uv_ref.at[idx_ref[t, r]], buf.at[slot, :, pl.ds(r, 1), :], sem.at[slot]).start()

    def wait(slot):
        pltpu.make_async_copy(fill_ref, buf.at[slot], sem.at[slot]).wait()

    eye = (lax.broadcasted_iota(jnp.int32, (PEER_SEL, PEER_SEL), 0)
           == lax.broadcasted_iota(jnp.int32, (PEER_SEL, PEER_SEL), 1))

    def compute(t, slot):
        hrow = h_ref[pl.ds(t, 1), :]
        acc = buf[slot, 0] * hrow[:, 0:LANE]
        for c in range(1, nt):
            acc = acc + buf[slot, c] * hrow[:, c * LANE:(c + 1) * LANE]
        s = jnp.sum(acc, axis=1, keepdims=True)
        gcol = jnp.sum(jnp.where(eye, g_ref[pl.ds(t, 1), :], 0.0), axis=1, keepdims=True)
        a = _gelu(s) * gcol
        o = jnp.concatenate([jnp.sum(buf[slot, nt + c] * a, axis=0, keepdims=True) for c in range(nt)], axis=1)
        xo = x_ref[pl.ds(t, 1), :] + m5_ref[0, 0] * o
        if final:
            xo = xo * lax.rsqrt(jnp.mean(xo * xo, axis=-1, keepdims=True) + EPS) * fg_ref[...]
        o_ref[pl.ds(t, 1), :] = xo

    for slot in range(PEER_NBUF):
        gather(slot, slot)

    def body(i, carry):
        for slot in range(PEER_NBUF):
            t = i * PEER_NBUF + slot
            wait(slot)
            compute(t, slot)

            @pl.when(t + PEER_NBUF < tb)
            def _():
                gather(t + PEER_NBUF, slot)
        return carry

    lax.fori_loop(0, tb // PEER_NBUF, body, 0)


def _experts_call(idx, g, h2, x, mod, row_of_block, final_g, uv, final, n_tc):
    n, d = h2.shape
    tb = PEER_TB
    tok = lambda w: pl.BlockSpec((tb, w), lambda i: (i, 0))
    fill = jnp.zeros((ROW_TILES, PEER_SEL, LANE), F32)
    return pl.pallas_call(
        functools.partial(_experts_kernel, final=final),
        out_shape=jax.ShapeDtypeStruct((n, d), F32),
        grid=(n_tc // tb,),
        in_specs=[pl.BlockSpec((tb, PEER_SEL), lambda i: (i, 0), memory_space=pltpu.SMEM),
                  tok(PEER_SEL), tok(d), tok(d),
                  _mod_spec(lambda i: row_of_block(i), 5),
                  _const_spec((1, d)),
                  pl.BlockSpec(memory_space=pl.ANY),
                  pl.BlockSpec(memory_space=pl.ANY)],
        out_specs=tok(d),
        scratch_shapes=[pltpu.VMEM((PEER_NBUF, ROW_TILES, PEER_SEL, LANE), F32),
                        pltpu.SemaphoreType.DMA((PEER_NBUF,))],
        compiler_params=_cparams("arbitrary"),
        name="peer_experts_final" if final else "peer_experts",
    )(idx, g, h2, x, mod, final_g, uv, fill)


SC_CORES = 2
SC_SUBCORES = 16
SC_LANES = 16
SC_WORKERS = SC_CORES * SC_SUBCORES
SC_ROWS = 32
SC_TOKENS = 11264


def _sc_gelu(x):
    y = math.sqrt(2.0 / math.pi) * (x + 0.044715 * (x * x * x))
    tanh_y = 1.0 - 2.0 / (jnp.exp(2.0 * y) + 1.0)
    return 0.5 * x * (1.0 + tanh_y)


def _sc_experts_call(idx, g, h2, uv2d, ns):
    n, d = h2.shape
    tok0 = n - ns
    per = ns // SC_WORKERS
    nv = d // SC_LANES
    mesh = plsc.VectorSubcoreMesh(core_axis_name="c", subcore_axis_name="s")

    def body(idx_hbm, g_hbm, h_hbm, uv_hbm, out_hbm, idx_v, g_v, h_v, rows_v, out_v, sem):
        wid = lax.axis_index("s") * SC_CORES + lax.axis_index("c")
        base = wid * per
        lane = lax.iota(jnp.int32, SC_LANES)
        zero = jnp.zeros((SC_LANES,), F32)

        def token(t, carry):
            tok = tok0 + base + t
            pltpu.sync_copy(idx_hbm.at[tok], idx_v)
            pltpu.sync_copy(g_hbm.at[tok], g_v)
            pltpu.sync_copy(h_hbm.at[tok], h_v)

            def clear(j, c):
                out_v[pl.ds(j * SC_LANES, SC_LANES)] = zero
                return c
            lax.fori_loop(0, nv, clear, 0)

            for ch in range(PEER_SEL // SC_ROWS):
                pltpu.async_copy(uv_hbm.at[idx_v.at[pl.ds(ch * SC_ROWS, SC_ROWS)]], rows_v, sem).wait()
                for grp in range(SC_ROWS // SC_LANES):
                    r0 = grp * SC_LANES

                    def ubody(j, accs):
                        hj = h_v[pl.ds(j * SC_LANES, SC_LANES)]
                        return tuple(accs[i] + rows_v[r0 + i, pl.ds(j * SC_LANES, SC_LANES)] * hj
                                     for i in range(SC_LANES))
                    accs = lax.fori_loop(0, nv, ubody, tuple(zero for _ in range(SC_LANES)))
                    s = zero
                    for i in range(SC_LANES):
                        s = jnp.where(lane == i, jnp.sum(accs[i]), s)
                    a = _sc_gelu(s) * g_v[pl.ds(ch * SC_ROWS + r0, SC_LANES)]
                    ab = [jnp.sum(jnp.where(lane == i, a, 0.0)) for i in range(SC_LANES)]

                    def vbody(cb, c):
                        col = cb * (SC_LANES * SC_LANES)
                        accs = [out_v[pl.ds(col + k * SC_LANES, SC_LANES)] for k in range(SC_LANES)]
                        for i in range(SC_LANES):
                            for k in range(SC_LANES):
                                accs[k] = accs[k] + rows_v[r0 + i, pl.ds(d + col + k * SC_LANES, SC_LANES)] * ab[i]
                        for k in range(SC_LANES):
                            out_v[pl.ds(col + k * SC_LANES, SC_LANES)] = accs[k]
                        return c
                    lax.fori_loop(0, d // (SC_LANES * SC_LANES), vbody, 0)
            pltpu.sync_copy(out_v, out_hbm.at[base + t])
            return carry

        lax.fori_loop(0, per, token, 0)

    return pl.kernel(
        body,
        out_type=jax.ShapeDtypeStruct((ns, d), F32),
        mesh=mesh,
        scratch_types=[pltpu.VMEM((PEER_SEL,), jnp.int32), pltpu.VMEM((PEER_SEL,), F32), pltpu.VMEM((d,), F32),
                       pltpu.VMEM((SC_ROWS, 2 * d), F32), pltpu.VMEM((d,), F32), pltpu.SemaphoreType.DMA],
        compiler_params=pltpu.CompilerParams(needs_layout_passes=False),
        name="peer_experts_sc",
    )(idx, g, h2, uv2d)


def _residual_kernel(x_ref, p_ref, m5_ref, fg_ref, prev_ref, o_ref, *, final):
    del prev_ref
    xo = x_ref[...] + m5_ref[0, 0] * p_ref[...]
    if final:
        xo = xo * lax.rsqrt(jnp.mean(xo * xo, axis=-1, keepdims=True) + EPS) * fg_ref[...]
    o_ref[...] = xo


def _residual_call(x, p, prev, mod, row_of_block, final_g, final):
    n, d = x.shape
    ns = p.shape[0]
    tb = PEER_TB
    first = (n - ns) // tb
    tail = pl.BlockSpec((tb, d), lambda i: (i + first, 0))
    return pl.pallas_call(
        functools.partial(_residual_kernel, final=final),
        out_shape=jax.ShapeDtypeStruct((n, d), F32),
        grid=(ns // tb,),
        in_specs=[tail, pl.BlockSpec((tb, d), lambda i: (i, 0)),
                  _mod_spec(lambda i: row_of_block(i + first), 5), _const_spec((1, d)),
                  pl.BlockSpec(memory_space=pl.ANY)],
        out_specs=tail,
        input_output_aliases={4: 0},
        compiler_params=_cparams("arbitrary"),
        name="peer_residual",
    )(x, p, mod, final_g, prev)


def _grid_sincos(n_tok, d):
    rows = n_tok // GRID_W
    n_freq = d // 4
    freq = 1.0 / (10000.0 ** (jnp.arange(n_freq, dtype=F32) / n_freq))
    r = jnp.repeat(jnp.arange(rows, dtype=F32), GRID_W)
    cc = jnp.tile(jnp.arange(GRID_W, dtype=F32), rows)
    ar = r[:, None] * freq[None, :]
    ac = cc[:, None] * freq[None, :]
    return jnp.concatenate([jnp.sin(ar), jnp.cos(ar), jnp.sin(ac), jnp.cos(ac)], axis=-1)


def _dft_tables(t_len):
    i = jnp.arange(t_len, dtype=jnp.int32)
    ph = (i[:, None] * i[None, :]) % t_len
    ang = ph.astype(F32) * (2.0 * math.pi / t_len)
    scale = 1.0 / math.sqrt(t_len * FT_DG)
    return (jnp.cos(ang) * scale).astype(BF16), (-jnp.sin(ang) * scale).astype(BF16)


def _channel_dft():
    i = jnp.arange(MIX_W, dtype=jnp.int32)
    same = (i[:, None] // FT_DG) == (i[None, :] // FT_DG)
    ph = ((i[:, None] % FT_DG) * (i[None, :] % FT_DG)) % FT_DG
    ang = ph.astype(F32) * (2.0 * math.pi / FT_DG)
    c = jnp.where(same, jnp.cos(ang), 0.0)
    s = jnp.where(same, jnp.sin(ang), 0.0)
    return jnp.concatenate([c, s], axis=1).astype(BF16)


def _pad_heads(a):
    lead = a.shape[:-1]
    a = a.reshape(lead + (ML_HEADS, ML_DH))
    a = jnp.pad(a, [(0, 0)] * len(lead) + [(0, 0), (0, HEAD_PAD - ML_DH)])
    return a.reshape(lead + (ML_PAD_W,))


def _layer_weights(l, w_in, ml_conv_w, ml_conv_b, ml_gate_b, ml_norm_g, gm_w_s, gm_b_s, w_branch):
    w = w_in[l]
    q, k, v = w[:, 0:256], w[:, 256:512], w[:, 512:768]
    gates, o = w[:, 768:784], w[:, 784:1040]
    w_ml = jnp.concatenate([_pad_heads(q), _pad_heads(k), _pad_heads(v), _pad_heads(o)], axis=1).astype(BF16)
    w_gc = jnp.pad(gates, ((0, 0), (0, LANE - 16))).astype(BF16)
    w_gr = gates.T.astype(BF16)
    w_rest = w[:, 1040:].astype(BF16)
    cw = ml_conv_w[l]
    conv_w = jnp.concatenate([_pad_heads(cw[:, :256]), _pad_heads(cw[:, 256:])], axis=1)
    cb = ml_conv_b[l]
    conv_b = jnp.concatenate([_pad_heads(cb[:256]), _pad_heads(cb[256:])])[None, :]
    gb = ml_gate_b[l]
    gbc = jnp.pad(gb, (0, LANE - 16))[None, :]
    gbr = jnp.broadcast_to(gb[:, None], (16, ML_CHUNK))
    mlg = _pad_heads(ml_norm_g[l].reshape(-1))[None, :]
    wcat = jnp.transpose(gm_w_s[l], (1, 0, 2)).reshape(GM_CHUNK, GM_GROUPS * GM_CHUNK).astype(BF16)
    bias_gm = jnp.repeat(gm_b_s[l].T, GM_DG, axis=1)
    wb = w_branch[l]
    wb0 = jnp.pad(wb[0].reshape(ML_HEADS, ML_DH, D_MODEL), ((0, 0), (0, HEAD_PAD - ML_DH), (0, 0)))
    wb0 = wb0.reshape(ML_PAD_W, D_MODEL).astype(BF16)
    return dict(w_ml=w_ml, w_gc=w_gc, w_gr=w_gr, w_rest=w_rest, conv_w=conv_w, conv_b=conv_b, gbc=gbc, gbr=gbr,
                mlg=mlg, wcat=wcat, bias_gm=bias_gm, wb0=wb0,
                wb1=wb[1].astype(BF16), wb2=wb[2].astype(BF16), wb3=wb[3].astype(BF16))


def kernel(x, c, ctx, c_ctx, w_ada, b_ada, norm1_g, norm2_g, w_in, ml_conv_w, ml_conv_b, ml_gate_b, ml_norm_g,
           gm_ln_g, gm_ln_b, gm_w_s, gm_b_s, cv_dw_w, cv_dw_b, cv_ln_g, cv_ln_b, w_branch, w_out, peer_w_q,
           peer_keys, peer_u, peer_v, final_norm_g):
    b_sz, n_tok, d = x.shape
    n_ctx = ctx.shape[1]
    depth = w_ada.shape[0]
    ctx_row = b_sz

    cond = jnp.zeros((16, d), F32).at[:b_sz].set(c).at[ctx_row].set(c_ctx)
    mod_all = _ada_call(cond, w_ada, b_ada)
    x = _addpos_call(x, _grid_sincos(n_tok, d))
    xc = ctx
    cs = _channel_dft()
    dft_x = _dft_tables(n_tok)
    dft_c = _dft_tables(n_ctx)
    final_g = final_norm_g[None, :]
    lat_row = lambda i: i
    ctx_rowf = lambda i: ctx_row

    def mixers(xs, l, lw, mod, row_fn, dft, cn, m, need_out):
        p_ml, gc, gr = _inproj_ml_call(xs, mod, row_fn, norm1_g[l][None], lw["w_ml"], lw["w_gc"], lw["w_gr"])
        qk = _dwconv_call(p_ml, 0, 2 * ML_PAD_W, lw["conv_w"], lw["conv_b"], True)
        res = _mlstm_call(qk, p_ml, gc, gr, lw["gbc"], lw["gbr"], cn, m, need_out)
        if not need_out:
            return None, res[0], res[1]
        hf, hb, cn, m = res
        zgm, zglu, fa, fb, gate = _inproj_rest_call(xs, mod, row_fn, norm1_g[l][None], lw["w_rest"],
                                                    gm_ln_g[l][None], gm_ln_b[l][None], lw["wcat"],
                                                    lw["bias_gm"], cs)
        cv = _dwconv_call(zglu, 0, MIX_W, cv_dw_w[l], cv_dw_b[l][None], False)
        zft = _fourier_call(dft[0], dft[1], fa, fb)
        xo, h2 = _merge_call(xs, mod, row_fn, hf, hb, p_ml, zgm, cv, zft, gate, lw["mlg"], cv_ln_g[l][None],
                             cv_ln_b[l][None], norm2_g[l][None], lw["wb0"], lw["wb1"], lw["wb2"], lw["wb3"],
                             w_out[l].astype(BF16))
        return (xo, h2), cn, m

    def peer(xo, h2, l, mod, row_of_block, uv2d, final, n_sc):
        bsz, t, _ = xo.shape
        n = bsz * t
        n_tc = n - n_sc
        h2f = h2.reshape(n, d)
        xf = xo.reshape(n, d)
        uv = uv2d.reshape(-1, ROW_TILES, 1, LANE)
        e, g = _route_call(h2f, peer_w_q[l].astype(BF16), peer_keys[l].astype(BF16))
        out = _experts_call(e, g, h2f, xf, mod, row_of_block, final_g, uv, final, n_tc)
        if n_sc:
            p_sc = _sc_experts_call(e, g, h2f, uv2d, n_sc)
            out = _residual_call(xf, p_sc, out, mod, row_of_block, final_g, final)
        return out.reshape(bsz, t, d)

    for l in range(depth):
        last = l == depth - 1
        lw = _layer_weights(l, w_in, ml_conv_w, ml_conv_b, ml_gate_b, ml_norm_g, gm_w_s, gm_b_s, w_branch)
        mod = mod_all[l].reshape(16, 6, 1, d)
        uv = jnp.concatenate([peer_u[l], peer_v[l]], axis=1)
        cn0 = jnp.zeros((b_sz, 2 * ML_HEADS, HEAD_PAD, HEAD_PAD), F32)
        m0 = jnp.zeros((b_sz, 2 * ML_HEADS, 1, LANE), F32)
        res_c, cn, m = mixers(xc, l, lw, mod, ctx_rowf, dft_c, cn0, m0, not last)
        res_x, _, _ = mixers(x, l, lw, mod, lat_row, dft_x, cn, m, True)
        blocks_per_batch = n_tok // PEER_TB
        n_sc = SC_TOKENS if b_sz * n_tok >= 2 * SC_TOKENS else 0
        x = peer(res_x[0], res_x[1], l, mod, lambda i: i // blocks_per_batch, uv, last, n_sc)
        if not last:
            xc = peer(res_c[0], res_c[1], l, mod, lambda i: ctx_row, uv, False, 0)
    return x
```

```python
import functools
import math

import jax
import jax.numpy as jnp
from jax import lax
from jax.experimental import pallas as pl
from jax.experimental.pallas import tpu as pltpu
from jax.experimental.pallas import tpu_sc as plsc

F32 = jnp.float32
BF16 = jnp.bfloat16
HIGHEST = lax.Precision.HIGHEST

D_MODEL = 1024
GRID_W = 64
EPS = 1e-6
MIX_W = 256
ML_HEADS = 4
ML_DH = 64
ML_CHUNK = 128
GM_CHUNK = 128
GM_GROUPS = 4
GM_DG = 64
CV_K = 31
ML_CONV = 3
FT_GROUPS = 4
FT_DG = 64
PEER_HEADS = 8
PEER_TOPK = 16
N_KEYS = 128
PEER_SEL = PEER_HEADS * PEER_TOPK

LANE = 128
HEAD_PAD = LANE
ML_PAD_W = ML_HEADS * HEAD_PAD
CONV_HALO = 16
VMEM_LIMIT = 56 * 1024 * 1024


def _cparams(*sem):
    return pltpu.CompilerParams(dimension_semantics=sem, vmem_limit_bytes=VMEM_LIMIT)


def _sigmoid(x):
    return 1.0 / (1.0 + jnp.exp(-x))


def _silu(x):
    return x * _sigmoid(x)


def _gelu(x):
    return 0.5 * x * (1.0 + jnp.tanh(math.sqrt(2.0 / math.pi) * (x + 0.044715 * (x * x * x))))


def _log_sigmoid(x):
    return jnp.minimum(x, 0.0) - jnp.log(1.0 + jnp.exp(-jnp.abs(x)))


def _rms_mod(x, g, shift, scale):
    y = x * lax.rsqrt(jnp.mean(x * x, axis=-1, keepdims=True) + EPS) * g
    return y * (1.0 + scale) + shift


def _layernorm(x, g, b):
    mu = jnp.mean(x, axis=-1, keepdims=True)
    xc = x - mu
    return xc * lax.rsqrt(jnp.mean(xc * xc, axis=-1, keepdims=True) + EPS) * g + b


def _dot(a, b):
    return jnp.dot(a, b, preferred_element_type=F32)


def _dot_nt(a, b):
    return lax.dot_general(a, b, (((1,), (1,)), ((), ())), preferred_element_type=F32)


def _dot_exact(a, b):
    return jnp.dot(a, b, preferred_element_type=F32, precision=HIGHEST)


def _ada_kernel(s_ref, w_ref, b_ref, o_ref):
    s = _silu(s_ref[...])
    o_ref[0] = _dot_exact(s, w_ref[0]) + b_ref[0]


def _ada_call(cond, w_ada, b_ada):
    depth, d, n6 = w_ada.shape
    rows = cond.shape[0]
    tn = 1536
    return pl.pallas_call(
        _ada_kernel,
        out_shape=jax.ShapeDtypeStruct((depth, rows, n6), F32),
        grid=(depth, n6 // tn),
        in_specs=[pl.BlockSpec((rows, d), lambda l, j: (0, 0)),
                  pl.BlockSpec((1, d, tn), lambda l, j: (l, 0, j)),
                  pl.BlockSpec((1, 1, tn), lambda l, j: (l, 0, j))],
        out_specs=pl.BlockSpec((1, rows, tn), lambda l, j: (l, 0, j)),
        compiler_params=_cparams("arbitrary", "arbitrary"),
        name="ada_mod",
    )(cond, w_ada, b_ada.reshape(depth, 1, n6))


def _addpos_kernel(x_ref, p_ref, o_ref):
    o_ref[0] = x_ref[0] + p_ref[...]


def _addpos_call(x, pos):
    b, t, d = x.shape
    tm = min(t, 512)
    return pl.pallas_call(
        _addpos_kernel,
        out_shape=jax.ShapeDtypeStruct(x.shape, F32),
        grid=(t // tm, b),
        in_specs=[pl.BlockSpec((1, tm, d), lambda i, j: (j, i, 0)),
                  pl.BlockSpec((tm, d), lambda i, j: (i, 0))],
        out_specs=pl.BlockSpec((1, tm, d), lambda i, j: (j, i, 0)),
        compiler_params=_cparams("arbitrary", "arbitrary"),
        name="add_pos",
    )(x, pos)


def _mod_spec(row_fn, k):
    return pl.BlockSpec((1, 1, 1, D_MODEL), lambda *g: (row_fn(*g), k, 0, 0))


def _const_spec(shape):
    nd = len(shape)
    return pl.BlockSpec(shape, lambda *g: (0,) * nd)


def _inproj_ml_kernel(x_ref, sh_ref, sc_ref, g_ref, w_ref, wgc_ref, wgr_ref, p_ref, gc_ref, gr_ref):
    h = _rms_mod(x_ref[0], g_ref[...], sh_ref[0, 0], sc_ref[0, 0]).astype(BF16)
    p_ref[0] = _dot(h, w_ref[...])
    gc_ref[0] = _dot(h, wgc_ref[...])
    gr_ref[0] = _dot_nt(wgr_ref[...], h)


def _inproj_ml_call(x, mod, row_fn, norm_g, w_ml, w_gc, w_gr):
    b, t, d = x.shape
    tm = 256
    n_ml = w_ml.shape[1]
    return pl.pallas_call(
        _inproj_ml_kernel,
        out_shape=(jax.ShapeDtypeStruct((b, t, n_ml), F32),
                   jax.ShapeDtypeStruct((b, t, LANE), F32),
                   jax.ShapeDtypeStruct((b, 16, t), F32)),
        grid=(b, t // tm),
        in_specs=[pl.BlockSpec((1, tm, d), lambda i, j: (i, j, 0)),
                  _mod_spec(lambda i, j: row_fn(i), 0),
                  _mod_spec(lambda i, j: row_fn(i), 1),
                  _const_spec((1, d)),
                  _const_spec(w_ml.shape),
                  _const_spec(w_gc.shape),
                  _const_spec(w_gr.shape)],
        out_specs=(pl.BlockSpec((1, tm, n_ml), lambda i, j: (i, j, 0)),
                   pl.BlockSpec((1, tm, LANE), lambda i, j: (i, j, 0)),
                   pl.BlockSpec((1, 16, tm), lambda i, j: (i, 0, j))),
        compiler_params=_cparams("arbitrary", "arbitrary"),
        name="inproj_ml",
    )(x, mod, mod, norm_g, w_ml, w_gc, w_gr)


def _inproj_rest_kernel(x_ref, sh_ref, sc_ref, g_ref, w_ref, lng_ref, lnb_ref, wcat_ref, bgm_ref, cs_ref,
                        zgm_ref, zglu_ref, fa_ref, fb_ref, gate_ref):
    tm = x_ref.shape[1]
    h = _rms_mod(x_ref[0], g_ref[...], sh_ref[0, 0], sc_ref[0, 0]).astype(BF16)
    z = _gelu(_dot(h, w_ref[:, 0:2 * MIX_W]))
    u = z[:, :MIX_W]
    v = _layernorm(z[:, MIX_W:], lng_ref[...], lnb_ref[...])
    grp = lax.broadcasted_iota(jnp.int32, (GM_CHUNK, MIX_W), 1) // GM_DG
    for j in range(tm // GM_CHUNK):
        vc = v[j * GM_CHUNK:(j + 1) * GM_CHUNK]
        vstack = jnp.concatenate([jnp.where(grp == g, vc, 0.0) for g in range(GM_GROUPS)], axis=0).astype(BF16)
        s = _dot(wcat_ref[...], vstack) + bgm_ref[...]
        zgm_ref[0, j * GM_CHUNK:(j + 1) * GM_CHUNK, :] = u[j * GM_CHUNK:(j + 1) * GM_CHUNK] * s
    pc = _dot(h, w_ref[:, 2 * MIX_W:4 * MIX_W])
    zglu_ref[0] = pc[:, :MIX_W] * _sigmoid(pc[:, MIX_W:])
    pf = _dot(h, w_ref[:, 4 * MIX_W:5 * MIX_W]).astype(BF16)
    ab = _dot(pf, cs_ref[...])
    fa_ref[...] = ab[:, :MIX_W].astype(BF16)
    fb_ref[...] = ab[:, MIX_W:].astype(BF16)
    for i in range(4):
        lo = 5 * MIX_W + i * D_MODEL
        gate_ref[0, :, i * D_MODEL:(i + 1) * D_MODEL] = _sigmoid(_dot(h, w_ref[:, lo:lo + D_MODEL])).astype(BF16)


def _inproj_rest_call(x, mod, row_fn, norm_g, w_rest, gm_ln_g, gm_ln_b, wcat, bias_gm, cs):
    b, t, d = x.shape
    tm = 256
    return pl.pallas_call(
        _inproj_rest_kernel,
        out_shape=(jax.ShapeDtypeStruct((b, t, MIX_W), F32),
                   jax.ShapeDtypeStruct((b, t, MIX_W), F32),
                   jax.ShapeDtypeStruct((t, b * MIX_W), BF16),
                   jax.ShapeDtypeStruct((t, b * MIX_W), BF16),
                   jax.ShapeDtypeStruct((b, t, 4 * D_MODEL), BF16)),
        grid=(b, t // tm),
        in_specs=[pl.BlockSpec((1, tm, d), lambda i, j: (i, j, 0)),
                  _mod_spec(lambda i, j: row_fn(i), 0),
                  _mod_spec(lambda i, j: row_fn(i), 1),
                  _const_spec((1, d)),
                  _const_spec(w_rest.shape),
                  _const_spec((1, MIX_W)), _const_spec((1, MIX_W)),
                  _const_spec(wcat.shape), _const_spec(bias_gm.shape), _const_spec(cs.shape)],
        out_specs=(pl.BlockSpec((1, tm, MIX_W), lambda i, j: (i, j, 0)),
                   pl.BlockSpec((1, tm, MIX_W), lambda i, j: (i, j, 0)),
                   pl.BlockSpec((tm, MIX_W), lambda i, j: (j, i)),
                   pl.BlockSpec((tm, MIX_W), lambda i, j: (j, i)),
                   pl.BlockSpec((1, tm, 4 * D_MODEL), lambda i, j: (i, j, 0))),
        compiler_params=_cparams("arbitrary", "arbitrary"),
        name="inproj_rest",
    )(x, mod, mod, norm_g, w_rest, gm_ln_g, gm_ln_b, wcat, bias_gm, cs)


def _dwconv_kernel(x_ref, prev_ref, next_ref, w_ref, b_ref, o_ref, win_ref, *, taps, act):
    i = pl.program_id(2)
    last = pl.num_programs(2) - 1
    rc = x_ref.shape[1]
    win_ref[0:CONV_HALO, :] = jnp.where(i > 0, prev_ref[0], 0.0)
    win_ref[CONV_HALO:CONV_HALO + rc, :] = x_ref[0]
    win_ref[CONV_HALO + rc:, :] = jnp.where(i < last, next_ref[0], 0.0)
    sub = 128
    for r in range(rc // sub):
        acc = jnp.broadcast_to(b_ref[...], (sub, LANE))
        for j in range(taps):
            lo = CONV_HALO + r * sub + j - taps // 2
            acc = acc + w_ref[j:j + 1, :] * win_ref[lo:lo + sub, :]
        if act:
            acc = _silu(acc)
        o_ref[0, r * sub:(r + 1) * sub, :] = acc


def _dwconv_call(x, col0, ncol, w, bias, act):
    b, t, _ = x.shape
    taps = w.shape[0]
    rc = min(t, 512)
    c0 = col0 // LANE
    hb = rc // CONV_HALO
    nh = t // CONV_HALO
    return pl.pallas_call(
        functools.partial(_dwconv_kernel, taps=taps, act=act),
        out_shape=jax.ShapeDtypeStruct((b, t, ncol), F32),
        grid=(b, ncol // LANE, t // rc),
        in_specs=[pl.BlockSpec((1, rc, LANE), lambda bi, c, i: (bi, i, c0 + c)),
                  pl.BlockSpec((1, CONV_HALO, LANE), lambda bi, c, i: (bi, jnp.maximum(i * hb - 1, 0), c0 + c)),
                  pl.BlockSpec((1, CONV_HALO, LANE), lambda bi, c, i: (bi, jnp.minimum((i + 1) * hb, nh - 1), c0 + c)),
                  pl.BlockSpec((taps, LANE), lambda bi, c, i: (0, c)),
                  pl.BlockSpec((1, LANE), lambda bi, c, i: (0, c))],
        out_specs=pl.BlockSpec((1, rc, LANE), lambda bi, c, i: (bi, i, c)),
        scratch_shapes=[pltpu.VMEM((rc + 2 * CONV_HALO, LANE), F32)],
        compiler_params=_cparams("arbitrary", "arbitrary", "arbitrary"),
        name="dwconv_k%d" % taps,
    )(x, x, x, w, bias)


def _mlstm_kernel(*refs, need_out):
    (qkf_ref, qkb_ref, vf_ref, vb_ref, gcf_ref, gcb_ref, grf_ref, grb_ref, gbc_ref, gbr_ref,
     cn_in_ref, m_in_ref) = refs[:12]
    if need_out:
        hf_ref, hb_ref, cn_out_ref, m_out_ref, cn_s, m_s = refs[12:]
    else:
        cn_out_ref, m_out_ref, cn_s, m_s = refs[12:]
        hf_ref = hb_ref = None
    j = pl.program_id(1)
    last = pl.num_programs(1) - 1
    L = ML_CHUNK

    @pl.when(j == 0)
    def _():
        cn_s[...] = cn_in_ref[0]
        m_s[...] = m_in_ref[0]

    row = lax.broadcasted_iota(jnp.int32, (L, L), 0)
    col = lax.broadcasted_iota(jnp.int32, (L, L), 1)
    lane = lax.broadcasted_iota(jnp.int32, (1, HEAD_PAD), 1)
    one_at_dh = (lane == ML_DH).astype(F32)
    keep_dh = (lane < ML_DH).astype(F32)

    for direction in range(2):
        qk_ref, v_ref, gc_ref, gr_ref, h_ref = ((qkf_ref, vf_ref, gcf_ref, grf_ref, hf_ref) if direction == 0
                                                else (qkb_ref, vb_ref, gcb_ref, grb_ref, hb_ref))
        visible = (col <= row) if direction == 0 else (col >= row)
        tri_c = visible.astype(F32)
        tri_r = ((row <= col) if direction == 0 else (row >= col)).astype(F32)
        gcv = gc_ref[0] + gbc_ref[...]
        grv = gr_ref[0] + gbr_ref[...]
        lf_c = _log_sigmoid(gcv)
        lf_r = _log_sigmoid(grv)
        cum_c = _dot_exact(tri_c, lf_c)
        cum_r = _dot_exact(lf_r, tri_r)
        tot_c = jnp.sum(lf_c, axis=0, keepdims=True)
        for hd in range(ML_HEADS):
            g = direction * ML_HEADS + hd
            gi_col = g
            gf_col = 2 * ML_HEADS + g
            li_r = grv[gi_col:gi_col + 1, :]
            li_c = gcv[:, gi_col:gi_col + 1]
            cc = cum_c[:, gf_col:gf_col + 1]
            cr = cum_r[gf_col:gf_col + 1, :]
            tot = tot_c[:, gf_col:gf_col + 1]
            m_prev = m_s[g][:, 0:1]
            cn = cn_s[g]
            qh = qk_ref[0, :, hd * HEAD_PAD:(hd + 1) * HEAD_PAD]
            kh = qk_ref[0, :, ML_PAD_W + hd * HEAD_PAD:ML_PAD_W + (hd + 1) * HEAD_PAD] * (ML_DH ** -0.5)
            v1 = v_ref[0, :, hd * HEAD_PAD:(hd + 1) * HEAD_PAD] + one_at_dh
            w_r = tot - cr + li_r
            w_c = tot - cc + li_c
            m_new = jnp.maximum(tot + m_prev, jnp.max(w_r, axis=1, keepdims=True))
            ws_c = jnp.exp(w_c - m_new)
            decay = jnp.exp(tot + m_prev - m_new)
            if need_out:
                dlog = jnp.where(visible, cc - cr + li_r, -jnp.inf)
                inter = cc + m_prev
                m_t = jnp.maximum(inter, jnp.max(dlog, axis=1, keepdims=True))
                s = _dot_nt(qh.astype(BF16), kh.astype(BF16)) * jnp.exp(dlog - m_t)
                gi = jnp.exp(inter - m_t)
                nd = gi * _dot(qh.astype(BF16), cn.astype(BF16)) + _dot(s.astype(BF16), v1.astype(BF16))
                den = nd[:, ML_DH:ML_DH + 1]
                hh = nd / jnp.maximum(jnp.abs(den), jnp.exp(-m_t))
                h_ref[0, :, hd * HEAD_PAD:(hd + 1) * HEAD_PAD] = hh * keep_dh
            kw = (kh * ws_c).astype(BF16)
            upd = lax.dot_general(kw, v1.astype(BF16), (((0,), (0,)), ((), ())), preferred_element_type=F32)
            cn_s[g] = decay * cn + upd
            m_s[g] = jnp.broadcast_to(m_new, (1, LANE))

    @pl.when(j == last)
    def _():
        cn_out_ref[0] = cn_s[...]
        m_out_ref[0] = m_s[...]


def _mlstm_call(qk, p_ml, gc, gr, gbc, gbr, cn_in, m_in, need_out):
    b, t, _ = qk.shape
    L = ML_CHUNK
    nc = t // L
    g2 = 2 * ML_HEADS
    fwd = lambda i, j: (i, j, 0)
    bwd = lambda i, j: (i, nc - 1 - j, 0)
    in_specs = [pl.BlockSpec((1, L, 2 * ML_PAD_W), fwd),
                pl.BlockSpec((1, L, 2 * ML_PAD_W), bwd),
                pl.BlockSpec((1, L, ML_PAD_W), lambda i, j: (i, j, 2)),
                pl.BlockSpec((1, L, ML_PAD_W), lambda i, j: (i, nc - 1 - j, 2)),
                pl.BlockSpec((1, L, LANE), fwd),
                pl.BlockSpec((1, L, LANE), bwd),
                pl.BlockSpec((1, 16, L), lambda i, j: (i, 0, j)),
                pl.BlockSpec((1, 16, L), lambda i, j: (i, 0, nc - 1 - j)),
                _const_spec((1, LANE)),
                _const_spec((16, L)),
                pl.BlockSpec((1, g2, HEAD_PAD, HEAD_PAD), lambda i, j: (i, 0, 0, 0)),
                pl.BlockSpec((1, g2, 1, LANE), lambda i, j: (i, 0, 0, 0))]
    state_shapes = (jax.ShapeDtypeStruct((b, g2, HEAD_PAD, HEAD_PAD), F32),
                    jax.ShapeDtypeStruct((b, g2, 1, LANE), F32))
    state_specs = (pl.BlockSpec((1, g2, HEAD_PAD, HEAD_PAD), lambda i, j: (i, 0, 0, 0)),
                   pl.BlockSpec((1, g2, 1, LANE), lambda i, j: (i, 0, 0, 0)))
    if need_out:
        out_shape = (jax.ShapeDtypeStruct((b, t, ML_PAD_W), F32),
                     jax.ShapeDtypeStruct((b, t, ML_PAD_W), F32)) + state_shapes
        out_specs = (pl.BlockSpec((1, L, ML_PAD_W), fwd), pl.BlockSpec((1, L, ML_PAD_W), bwd)) + state_specs
    else:
        out_shape, out_specs = state_shapes, state_specs
    return pl.pallas_call(
        functools.partial(_mlstm_kernel, need_out=need_out),
        out_shape=out_shape,
        grid=(b, nc),
        in_specs=in_specs,
        out_specs=out_specs,
        scratch_shapes=[pltpu.VMEM((g2, HEAD_PAD, HEAD_PAD), F32), pltpu.VMEM((g2, 1, LANE), F32)],
        compiler_params=_cparams("arbitrary", "arbitrary"),
        name="mlstm_out" if need_out else "mlstm_state",
    )(qk, qk, p_ml, p_ml, gc, gc, gr, gr, gbc, gbr, cn_in, m_in)


def _fourier_kernel(ct_ref, st_ref, a_ref, b_ref, o_ref):
    @pl.when(pl.program_id(1) == 0)
    def _():
        o_ref[...] = jnp.zeros_like(o_ref)
    o_ref[...] += _dot(ct_ref[...], a_ref[...]) + _dot(st_ref[...], b_ref[...])


def _fourier_call(ct, mst, fa, fb):
    t, n = fa.shape
    tm = min(t, 512)
    return pl.pallas_call(
        _fourier_kernel,
        out_shape=jax.ShapeDtypeStruct((t, n), F32),
        grid=(t // tm, t // tm),
        in_specs=[pl.BlockSpec((tm, tm), lambda i, k: (i, k)),
                  pl.BlockSpec((tm, tm), lambda i, k: (i, k)),
                  pl.BlockSpec((tm, n), lambda i, k: (k, 0)),
                  pl.BlockSpec((tm, n), lambda i, k: (k, 0))],
        out_specs=pl.BlockSpec((tm, n), lambda i, k: (i, 0)),
        compiler_params=_cparams("arbitrary", "arbitrary"),
        name="fourier_pos",
    )(ct, mst, fa, fb)


def _merge_kernel(x_ref, m2_ref, m3_ref, m4_ref, hf_ref, hb_ref, o_ref, zgm_ref, cv_ref, zft_ref, gate_ref,
                  mlg_ref, cvg_ref, cvb_ref, n2g_ref, wb0_ref, wb1_ref, wb2_ref, wb3_ref, wout_ref,
                  xo_ref, h2_ref):
    hsum = hf_ref[0] + hb_ref[0]
    parts = []
    for hd in range(ML_HEADS):
        blk = hsum[:, hd * HEAD_PAD:(hd + 1) * HEAD_PAD]
        ms = jnp.sum(blk * blk, axis=-1, keepdims=True) * (1.0 / ML_DH)
        parts.append(blk * lax.rsqrt(ms + EPS))
    z_ml = jnp.concatenate(parts, axis=1) * mlg_ref[...] * _sigmoid(o_ref[0])
    z_cv = _silu(_layernorm(cv_ref[0], cvg_ref[...], cvb_ref[...]))

    def gate(i):
        return gate_ref[0, :, i * D_MODEL:(i + 1) * D_MODEL].astype(F32)

    y = gate(0) * _dot(z_ml.astype(BF16), wb0_ref[...])
    y = y + gate(1) * _dot(zgm_ref[0].astype(BF16), wb1_ref[...])
    y = y + gate(2) * _dot(z_cv.astype(BF16), wb2_ref[...])
    y = y + gate(3) * _dot(zft_ref[...].astype(BF16), wb3_ref[...])
    xo = x_ref[0] + m2_ref[0, 0] * _dot(y.astype(BF16), wout_ref[...])
    xo_ref[0] = xo
    h2_ref[0] = _rms_mod(xo, n2g_ref[...], m3_ref[0, 0], m4_ref[0, 0])


def _merge_call(x, mod, row_fn, hf, hb, p_ml, zgm, cv, zft, gate, mlg, cvg, cvb, n2g, wb0, wb1, wb2, wb3, wout):
    b, t, d = x.shape
    tm = 256
    tok = lambda w: pl.BlockSpec((1, tm, w), lambda i, j: (i, j, 0))
    rf = lambda i, j: row_fn(i)
    return pl.pallas_call(
        _merge_kernel,
        out_shape=(jax.ShapeDtypeStruct((b, t, d), F32), jax.ShapeDtypeStruct((b, t, d), F32)),
        grid=(b, t // tm),
        in_specs=[tok(d), _mod_spec(rf, 2), _mod_spec(rf, 3), _mod_spec(rf, 4),
                  tok(ML_PAD_W), tok(ML_PAD_W),
                  pl.BlockSpec((1, tm, ML_PAD_W), lambda i, j: (i, j, 3)),
                  tok(MIX_W), tok(MIX_W),
                  pl.BlockSpec((tm, MIX_W), lambda i, j: (j, i)),
                  tok(4 * d),
                  _const_spec((1, ML_PAD_W)), _const_spec((1, MIX_W)), _const_spec((1, MIX_W)), _const_spec((1, d)),
                  _const_spec(wb0.shape), _const_spec(wb1.shape), _const_spec(wb2.shape), _const_spec(wb3.shape),
                  _const_spec(wout.shape)],
        out_specs=(tok(d), tok(d)),
        compiler_params=_cparams("arbitrary", "arbitrary"),
        name="merge",
    )(x, mod, mod, mod, hf, hb, p_ml, zgm, cv, zft, gate, mlg, cvg, cvb, n2g, wb0, wb1, wb2, wb3, wout)


def _top16_rows(s, ids):
    big = jnp.int32(2 ** 30)
    out_rows = lax.broadcasted_iota(jnp.int32, (PEER_TOPK, LANE), 0)
    vals = jnp.zeros((PEER_TOPK, LANE), F32)
    sel = jnp.zeros((PEER_TOPK, LANE), jnp.int32)
    for i in range(PEER_TOPK):
        m = jnp.max(s, axis=0, keepdims=True)
        pick = jnp.min(jnp.where(s == m, ids, big), axis=0, keepdims=True)
        s = jnp.where(ids == pick, -jnp.inf, s)
        vals = jnp.where(out_rows == i, m, vals)
        sel = jnp.where(out_rows == i, pick, sel)
    return vals, sel


def _peer_stage2(v1, i1, v2, i2):
    big = jnp.int32(2 ** 30)
    sub = lax.broadcasted_iota(jnp.int32, (8, LANE), 0)
    tiles = []
    for a in range(8):
        va = v1[a:a + 1, :]
        ea = i1[a:a + 1, :] * N_KEYS
        for half in range(2 if a == 0 else 1):
            b = sub + 8 * half
            ok = (a + 1) * (b + 1) <= PEER_TOPK
            sc = jnp.where(ok, va + v2[8 * half:8 * half + 8, :], -jnp.inf)
            tiles.append((sc, a * PEER_TOPK + b, ea + i2[8 * half:8 * half + 8, :]))
    tiles.append((v1[8:16, :] + v2[0:1, :], (sub + 8) * PEER_TOPK, i1[8:16, :] * N_KEYS + i2[0:1, :]))
    sc = jnp.concatenate([t[0] for t in tiles], axis=0)
    pos = jnp.concatenate([jnp.broadcast_to(t[1], (8, LANE)) for t in tiles], axis=0)
    eid = jnp.concatenate([t[2] for t in tiles], axis=0)
    out_rows = lax.broadcasted_iota(jnp.int32, (PEER_TOPK, LANE), 0)
    vals = jnp.zeros((PEER_TOPK, LANE), F32)
    sel = jnp.zeros((PEER_TOPK, LANE), jnp.int32)
    for i in range(PEER_TOPK):
        m = jnp.max(sc, axis=0, keepdims=True)
        pick = jnp.min(jnp.where(sc == m, pos, big), axis=0, keepdims=True)
        hit = pos == pick
        e = jnp.max(jnp.where(hit, eid, -1), axis=0, keepdims=True)
        sc = jnp.where(hit, -jnp.inf, sc)
        vals = jnp.where(out_rows == i, m, vals)
        sel = jnp.where(out_rows == i, e, sel)
    return vals, sel


def _route_kernel(h_ref, wq_ref, keys_ref, e_ref, g_ref):
    tm = h_ref.shape[0]
    q = _dot(h_ref[...].astype(BF16), wq_ref[...])
    key_ids = lax.broadcasted_iota(jnp.int32, (N_KEYS, LANE), 0)
    for c in range(tm // LANE):
        ids, ws = [], []
        for hd in range(PEER_HEADS):
            tops = []
            for p in range(2):
                lo = (hd * 2 + p) * N_KEYS
                qs = q[c * LANE:(c + 1) * LANE, lo:lo + N_KEYS].astype(BF16)
                s = _dot_nt(keys_ref[p], qs)
                tops.append(_top16_rows(s, key_ids))
            sc, eid = _peer_stage2(tops[0][0], tops[0][1], tops[1][0], tops[1][1])
            w = jnp.exp(sc - sc[0:1, :])
            ids.append(eid)
            ws.append(w / jnp.sum(w, axis=0, keepdims=True))
        e_ref[c * LANE:(c + 1) * LANE, :] = jnp.concatenate(ids, axis=0).T
        g_ref[c * LANE:(c + 1) * LANE, :] = jnp.concatenate(ws, axis=0).T


def _route_call(h2, wq, keys):
    n, d = h2.shape
    tm = 256
    return pl.pallas_call(
        _route_kernel,
        out_shape=(jax.ShapeDtypeStruct((n, PEER_SEL), jnp.int32), jax.ShapeDtypeStruct((n, PEER_SEL), F32)),
        grid=(n // tm,),
        in_specs=[pl.BlockSpec((tm, d), lambda i: (i, 0)), _const_spec(wq.shape), _const_spec(keys.shape)],
        out_specs=(pl.BlockSpec((tm, PEER_SEL), lambda i: (i, 0)), pl.BlockSpec((tm, PEER_SEL), lambda i: (i, 0))),
        compiler_params=_cparams("arbitrary"),
        name="peer_route",
    )(h2, wq, keys)


PEER_TB = 64
PEER_NBUF = 4
ROW_TILES = 2 * D_MODEL // LANE


def _experts_kernel(idx_ref, g_ref, h_ref, x_ref, m5_ref, fg_ref, uv_ref, fill_ref, o_ref, buf, sem, *, final):
    tb = h_ref.shape[0]
    nt = D_MODEL // LANE

    def gather(t, slot):
        for r in range(PEER_SEL):
            pltpu.make_async_copy(uv_ref.at[idx_ref[t, r]], buf.at[slot, :, pl.ds(r, 1), :], sem.at[slot]).start()

    def wait(slot):
        pltpu.make_async_copy(fill_ref, buf.at[slot], sem.at[slot]).wait()

    eye = (lax.broadcasted_iota(jnp.int32, (PEER_SEL, PEER_SEL), 0)
           == lax.broadcasted_iota(jnp.int32, (PEER_SEL, PEER_SEL), 1))

    def compute(t, slot):
        hrow = h_ref[pl.ds(t, 1), :]
        acc = buf[slot, 0] * hrow[:, 0:LANE]
        for c in range(1, nt):
            acc = acc + buf[slot, c] * hrow[:, c * LANE:(c + 1) * LANE]
        s = jnp.sum(acc, axis=1, keepdims=True)
        gcol = jnp.sum(jnp.where(eye, g_ref[pl.ds(t, 1), :], 0.0), axis=1, keepdims=True)
        a = _gelu(s) * gcol
        o = jnp.concatenate([jnp.sum(buf[slot, nt + c] * a, axis=0, keepdims=True) for c in range(nt)], axis=1)
        xo = x_ref[pl.ds(t, 1), :] + m5_ref[0, 0] * o
        if final:
            xo = xo * lax.rsqrt(jnp.mean(xo * xo, axis=-1, keepdims=True) + EPS) * fg_ref[...]
        o_ref[pl.ds(t, 1), :] = xo

    for slot in range(PEER_NBUF):
        gather(slot, slot)

    def body(i, carry):
        for slot in range(PEER_NBUF):
            t = i * PEER_NBUF + slot
            wait(slot)
            compute(t, slot)

            @pl.when(t + PEER_NBUF < tb)
            def _():
                gather(t + PEER_NBUF, slot)
        return carry

    lax.fori_loop(0, tb // PEER_NBUF, body, 0)


def _experts_call(idx, g, h2, x, mod, row_of_block, final_g, uv, final, n_tc):
    n, d = h2.shape
    tb = PEER_TB
    tok = lambda w: pl.BlockSpec((tb, w), lambda i: (i, 0))
    fill = jnp.zeros((ROW_TILES, PEER_SEL, LANE), F32)
    return pl.pallas_call(
        functools.partial(_experts_kernel, final=final),
        out_shape=jax.ShapeDtypeStruct((n, d), F32),
        grid=(n_tc // tb,),
        in_specs=[pl.BlockSpec((tb, PEER_SEL), lambda i: (i, 0), memory_space=pltpu.SMEM),
                  tok(PEER_SEL), tok(d), tok(d),
                  _mod_spec(lambda i: row_of_block(i), 5),
                  _const_spec((1, d)),
                  pl.BlockSpec(memory_space=pl.ANY),
                  pl.BlockSpec(memory_space=pl.ANY)],
        out_specs=tok(d),
        scratch_shapes=[pltpu.VMEM((PEER_NBUF, ROW_TILES, PEER_SEL, LANE), F32),
                        pltpu.SemaphoreType.DMA((PEER_NBUF,))],
        compiler_params=_cparams("arbitrary"),
        name="peer_experts_final" if final else "peer_experts",
    )(idx, g, h2, x, mod, final_g, uv, fill)


SC_CORES = 2
SC_SUBCORES = 16
SC_LANES = 16
SC_WORKERS = SC_CORES * SC_SUBCORES
SC_ROWS = 16
SC_TBATCH = 8
SC_GROUPS = 2
SC_TOKENS = 10752


def _sc_gelu(x):
    y = math.sqrt(2.0 / math.pi) * (x + 0.044715 * (x * x * x))
    tanh_y = 1.0 - 2.0 / (jnp.exp(2.0 * y) + 1.0)
    return 0.5 * x * (1.0 + tanh_y)


def _sc_experts_call(idx, g, h2, uv2d, ns):
    n, d = h2.shape
    tok0 = n - ns
    per = ns // SC_WORKERS
    nv = d // SC_LANES
    mesh = plsc.VectorSubcoreMesh(core_axis_name="c", subcore_axis_name="s")

    nch = PEER_SEL // SC_ROWS
    nq = SC_TBATCH * nch
    assert SC_ROWS == SC_LANES and nch & (nch - 1) == 0 and per % SC_TBATCH == 0
    shift = nch.bit_length() - 1

    def body(idx_hbm, g_hbm, h_hbm, uv_hbm, out_hbm, idx_v, g_v, h_v, rows_a, rows_b, out_v, sem_a, sem_b):
        wid = lax.axis_index("s") * SC_CORES + lax.axis_index("c")
        base = wid * per
        lane = lax.iota(jnp.int32, SC_LANES)
        zero = jnp.zeros((SC_LANES,), F32)

        def gather(q, rows, sem):
            tl = lax.shift_right_logical(q, shift)
            ch = jnp.bitwise_and(q, nch - 1)
            ids = idx_v[tl, pl.ds(ch * SC_ROWS, SC_ROWS)]
            return pltpu.make_async_copy(uv_hbm.at[ids], rows, sem)

        def compute(q, rows):
            tl = lax.shift_right_logical(q, shift)
            ch = jnp.bitwise_and(q, nch - 1)

            def ubody(j, accs):
                hj = h_v[tl, pl.ds(j * SC_LANES, SC_LANES)]
                return tuple(accs[i] + rows[i, pl.ds(j * SC_LANES, SC_LANES)] * hj for i in range(SC_LANES))
            accs = lax.fori_loop(0, nv, ubody, tuple(zero for _ in range(SC_LANES)))
            s = zero
            for i in range(SC_LANES):
                s = jnp.where(lane == i, jnp.sum(accs[i]), s)
            a = _sc_gelu(s) * g_v[tl, pl.ds(ch * SC_ROWS, SC_LANES)]
            ab = [jnp.sum(jnp.where(lane == i, a, 0.0)) for i in range(SC_LANES)]

            def vbody(cb, c):
                col = cb * (SC_LANES * SC_LANES)
                acc = [out_v[tl, pl.ds(col + k * SC_LANES, SC_LANES)] for k in range(SC_LANES)]
                for i in range(SC_LANES):
                    for k in range(SC_LANES):
                        acc[k] = acc[k] + rows[i, pl.ds(d + col + k * SC_LANES, SC_LANES)] * ab[i]
                for k in range(SC_LANES):
                    out_v[tl, pl.ds(col + k * SC_LANES, SC_LANES)] = acc[k]
                return c
            lax.fori_loop(0, d // (SC_LANES * SC_LANES), vbody, 0)

        def batch(bi, carry):
            t0 = base + bi * SC_TBATCH
            pltpu.sync_copy(idx_hbm.at[pl.ds(tok0 + t0, SC_TBATCH)], idx_v)
            pltpu.sync_copy(g_hbm.at[pl.ds(tok0 + t0, SC_TBATCH)], g_v)
            pltpu.sync_copy(h_hbm.at[pl.ds(tok0 + t0, SC_TBATCH)], h_v)

            def clear(j, c):
                for tl in range(SC_TBATCH):
                    out_v[tl, pl.ds(j * SC_LANES, SC_LANES)] = zero
                return c
            lax.fori_loop(0, nv, clear, 0)

            gather(0, rows_a, sem_a).start()

            def pair(p, c):
                q = 2 * p
                gather(q + 1, rows_b, sem_b).start()
                gather(q, rows_a, sem_a).wait()
                compute(q, rows_a)

                @pl.when(q + 2 < nq)
                def _():
                    gather(q + 2, rows_a, sem_a).start()
                gather(q + 1, rows_b, sem_b).wait()
                compute(q + 1, rows_b)
                return c
            lax.fori_loop(0, nq // 2, pair, 0)
            pltpu.sync_copy(out_v, out_hbm.at[pl.ds(t0, SC_TBATCH)])
            return carry

        lax.fori_loop(0, per // SC_TBATCH, batch, 0)

    return pl.kernel(
        body,
        out_type=jax.ShapeDtypeStruct((ns, d), F32),
        mesh=mesh,
        scratch_types=[pltpu.VMEM((SC_TBATCH, PEER_SEL), jnp.int32), pltpu.VMEM((SC_TBATCH, PEER_SEL), F32),
                       pltpu.VMEM((SC_TBATCH, d), F32),
                       pltpu.VMEM((SC_ROWS, 2 * d), F32), pltpu.VMEM((SC_ROWS, 2 * d), F32),
                       pltpu.VMEM((SC_TBATCH, d), F32),
                       pltpu.SemaphoreType.DMA, pltpu.SemaphoreType.DMA],
        compiler_params=pltpu.CompilerParams(needs_layout_passes=False),
        name="peer_experts_sc",
    )(idx, g, h2, uv2d)


def _residual_kernel(x_ref, p_ref, m5_ref, fg_ref, prev_ref, o_ref, *, final):
    del prev_ref
    xo = x_ref[...] + m5_ref[0, 0] * p_ref[...]
    if final:
        xo = xo * lax.rsqrt(jnp.mean(xo * xo, axis=-1, keepdims=True) + EPS) * fg_ref[...]
    o_ref[...] = xo


def _residual_call(x, p, prev, mod, row_of_block, final_g, final):
    n, d = x.shape
    ns = p.shape[0]
    tb = PEER_TB
    first = (n - ns) // tb
    tail = pl.BlockSpec((tb, d), lambda i: (i + first, 0))
    return pl.pallas_call(
        functools.partial(_residual_kernel, final=final),
        out_shape=jax.ShapeDtypeStruct((n, d), F32),
        grid=(ns // tb,),
        in_specs=[tail, pl.BlockSpec((tb, d), lambda i: (i, 0)),
                  _mod_spec(lambda i: row_of_block(i + first), 5), _const_spec((1, d)),
                  pl.BlockSpec(memory_space=pl.ANY)],
        out_specs=tail,
        input_output_aliases={4: 0},
        compiler_params=_cparams("arbitrary"),
        name="peer_residual",
    )(x, p, mod, final_g, prev)


def _grid_sincos(n_tok, d):
    rows = n_tok // GRID_W
    n_freq = d // 4
    freq = 1.0 / (10000.0 ** (jnp.arange(n_freq, dtype=F32) / n_freq))
    r = jnp.repeat(jnp.arange(rows, dtype=F32), GRID_W)
    cc = jnp.tile(jnp.arange(GRID_W, dtype=F32), rows)
    ar = r[:, None] * freq[None, :]
    ac = cc[:, None] * freq[None, :]
    return jnp.concatenate([jnp.sin(ar), jnp.cos(ar), jnp.sin(ac), jnp.cos(ac)], axis=-1)


def _dft_tables(t_len):
    i = jnp.arange(t_len, dtype=jnp.int32)
    ph = (i[:, None] * i[None, :]) % t_len
    ang = ph.astype(F32) * (2.0 * math.pi / t_len)
    scale = 1.0 / math.sqrt(t_len * FT_DG)
    return (jnp.cos(ang) * scale).astype(BF16), (-jnp.sin(ang) * scale).astype(BF16)


def _channel_dft():
    i = jnp.arange(MIX_W, dtype=jnp.int32)
    same = (i[:, None] // FT_DG) == (i[None, :] // FT_DG)
    ph = ((i[:, None] % FT_DG) * (i[None, :] % FT_DG)) % FT_DG
    ang = ph.astype(F32) * (2.0 * math.pi / FT_DG)
    c = jnp.where(same, jnp.cos(ang), 0.0)
    s = jnp.where(same, jnp.sin(ang), 0.0)
    return jnp.concatenate([c, s], axis=1).astype(BF16)


def _pad_heads(a):
    lead = a.shape[:-1]
    a = a.reshape(lead + (ML_HEADS, ML_DH))
    a = jnp.pad(a, [(0, 0)] * len(lead) + [(0, 0), (0, HEAD_PAD - ML_DH)])
    return a.reshape(lead + (ML_PAD_W,))


def _layer_weights(l, w_in, ml_conv_w, ml_conv_b, ml_gate_b, ml_norm_g, gm_w_s, gm_b_s, w_branch):
    w = w_in[l]
    q, k, v = w[:, 0:256], w[:, 256:512], w[:, 512:768]
    gates, o = w[:, 768:784], w[:, 784:1040]
    w_ml = jnp.concatenate([_pad_heads(q), _pad_heads(k), _pad_heads(v), _pad_heads(o)], axis=1).astype(BF16)
    w_gc = jnp.pad(gates, ((0, 0), (0, LANE - 16))).astype(BF16)
    w_gr = gates.T.astype(BF16)
    w_rest = w[:, 1040:].astype(BF16)
    cw = ml_conv_w[l]
    conv_w = jnp.concatenate([_pad_heads(cw[:, :256]), _pad_heads(cw[:, 256:])], axis=1)
    cb = ml_conv_b[l]
    conv_b = jnp.concatenate([_pad_heads(cb[:256]), _pad_heads(cb[256:])])[None, :]
    gb = ml_gate_b[l]
    gbc = jnp.pad(gb, (0, LANE - 16))[None, :]
    gbr = jnp.broadcast_to(gb[:, None], (16, ML_CHUNK))
    mlg = _pad_heads(ml_norm_g[l].reshape(-1))[None, :]
    wcat = jnp.transpose(gm_w_s[l], (1, 0, 2)).reshape(GM_CHUNK, GM_GROUPS * GM_CHUNK).astype(BF16)
    bias_gm = jnp.repeat(gm_b_s[l].T, GM_DG, axis=1)
    wb = w_branch[l]
    wb0 = jnp.pad(wb[0].reshape(ML_HEADS, ML_DH, D_MODEL), ((0, 0), (0, HEAD_PAD - ML_DH), (0, 0)))
    wb0 = wb0.reshape(ML_PAD_W, D_MODEL).astype(BF16)
    return dict(w_ml=w_ml, w_gc=w_gc, w_gr=w_gr, w_rest=w_rest, conv_w=conv_w, conv_b=conv_b, gbc=gbc, gbr=gbr,
                mlg=mlg, wcat=wcat, bias_gm=bias_gm, wb0=wb0,
                wb1=wb[1].astype(BF16), wb2=wb[2].astype(BF16), wb3=wb[3].astype(BF16))


def kernel(x, c, ctx, c_ctx, w_ada, b_ada, norm1_g, norm2_g, w_in, ml_conv_w, ml_conv_b, ml_gate_b, ml_norm_g,
           gm_ln_g, gm_ln_b, gm_w_s, gm_b_s, cv_dw_w, cv_dw_b, cv_ln_g, cv_ln_b, w_branch, w_out, peer_w_q,
           peer_keys, peer_u, peer_v, final_norm_g):
    b_sz, n_tok, d = x.shape
    n_ctx = ctx.shape[1]
    depth = w_ada.shape[0]
    ctx_row = b_sz

    cond = jnp.zeros((16, d), F32).at[:b_sz].set(c).at[ctx_row].set(c_ctx)
    mod_all = _ada_call(cond, w_ada, b_ada)
    x = _addpos_call(x, _grid_sincos(n_tok, d))
    xc = ctx
    cs = _channel_dft()
    dft_x = _dft_tables(n_tok)
    dft_c = _dft_tables(n_ctx)
    final_g = final_norm_g[None, :]
    lat_row = lambda i: i
    ctx_rowf = lambda i: ctx_row

    def mixers(xs, l, lw, mod, row_fn, dft, cn, m, need_out):
        p_ml, gc, gr = _inproj_ml_call(xs, mod, row_fn, norm1_g[l][None], lw["w_ml"], lw["w_gc"], lw["w_gr"])
        qk = _dwconv_call(p_ml, 0, 2 * ML_PAD_W, lw["conv_w"], lw["conv_b"], True)
        res = _mlstm_call(qk, p_ml, gc, gr, lw["gbc"], lw["gbr"], cn, m, need_out)
        if not need_out:
            return None, res[0], res[1]
        hf, hb, cn, m = res
        zgm, zglu, fa, fb, gate = _inproj_rest_call(xs, mod, row_fn, norm1_g[l][None], lw["w_rest"],
                                                    gm_ln_g[l][None], gm_ln_b[l][None], lw["wcat"],
                                                    lw["bias_gm"], cs)
        cv = _dwconv_call(zglu, 0, MIX_W, cv_dw_w[l], cv_dw_b[l][None], False)
        zft = _fourier_call(dft[0], dft[1], fa, fb)
        xo, h2 = _merge_call(xs, mod, row_fn, hf, hb, p_ml, zgm, cv, zft, gate, lw["mlg"], cv_ln_g[l][None],
                             cv_ln_b[l][None], norm2_g[l][None], lw["wb0"], lw["wb1"], lw["wb2"], lw["wb3"],
                             w_out[l].astype(BF16))
        return (xo, h2), cn, m

    def peer(xo, h2, l, mod, row_of_block, uv_pair, final, n_sc):
        bsz, t, _ = xo.shape
        n = bsz * t
        n_tc = n - n_sc
        h2f = h2.reshape(n, d)
        xf = xo.reshape(n, d)
        uv, uv2d = uv_pair
        e, g = _route_call(h2f, peer_w_q[l].astype(BF16), peer_keys[l].astype(BF16))
        out = _experts_call(e, g, h2f, xf, mod, row_of_block, final_g, uv, final, n_tc) if n_tc else xf
        if n_sc:
            p_sc = _sc_experts_call(e, g, h2f, uv2d, n_sc)
            out = _residual_call(xf, p_sc, out, mod, row_of_block, final_g, final)
        return out.reshape(bsz, t, d)

    groups = SC_GROUPS if b_sz % SC_GROUPS == 0 else 1
    gb = b_sz // groups
    use_sc = (gb * n_tok >= SC_TOKENS + PEER_TB and (gb * n_tok - SC_TOKENS) % PEER_TB == 0
              and (b_sz * n_ctx) % (SC_WORKERS * SC_TBATCH) == 0)
    blocks_per_batch = n_tok // PEER_TB
    xs = [x[i * gb:(i + 1) * gb] for i in range(groups)]

    for l in range(depth):
        last = l == depth - 1
        lw = _layer_weights(l, w_in, ml_conv_w, ml_conv_b, ml_gate_b, ml_norm_g, gm_w_s, gm_b_s, w_branch)
        mod = mod_all[l].reshape(16, 6, 1, d)
        half_tiles = (-1, ROW_TILES // 2, 1, LANE)
        uv = (jnp.concatenate([peer_u[l].reshape(half_tiles), peer_v[l].reshape(half_tiles)], axis=1),
              jnp.concatenate([peer_u[l], peer_v[l]], axis=1))
        cn0 = jnp.zeros((b_sz, 2 * ML_HEADS, HEAD_PAD, HEAD_PAD), F32)
        m0 = jnp.zeros((b_sz, 2 * ML_HEADS, 1, LANE), F32)
        res_c, cn, m = mixers(xc, l, lw, mod, ctx_rowf, dft_c, cn0, m0, not last)
        if not last:
            xc = peer(res_c[0], res_c[1], l, mod, lambda i: ctx_row, uv, False, b_sz * n_ctx if use_sc else 0)
        for gi in range(groups):
            b0 = gi * gb
            res_x, _, _ = mixers(xs[gi], l, lw, mod, lambda i, b0=b0: b0 + i, dft_x, cn[b0:b0 + gb], m[b0:b0 + gb], True)
            xs[gi] = peer(res_x[0], res_x[1], l, mod, lambda i, b0=b0: b0 + i // blocks_per_batch, uv, last,
                          SC_TOKENS if use_sc else 0)
    return jnp.concatenate(xs, axis=0)
```

```python
import functools
import math

import jax
import jax.numpy as jnp
from jax import lax
from jax.experimental import pallas as pl
from jax.experimental.pallas import tpu as pltpu
from jax.experimental.pallas import tpu_sc as plsc

F32 = jnp.float32
BF16 = jnp.bfloat16
HIGHEST = lax.Precision.HIGHEST

D_MODEL = 1024
GRID_W = 64
EPS = 1e-6
MIX_W = 256
ML_HEADS = 4
ML_DH = 64
ML_CHUNK = 128
GM_CHUNK = 128
GM_GROUPS = 4
GM_DG = 64
CV_K = 31
ML_CONV = 3
FT_GROUPS = 4
FT_DG = 64
PEER_HEADS = 8
PEER_TOPK = 16
N_KEYS = 128
PEER_SEL = PEER_HEADS * PEER_TOPK

LANE = 128
HEAD_PAD = LANE
ML_PAD_W = ML_HEADS * HEAD_PAD
CONV_HALO = 16
VMEM_LIMIT = 56 * 1024 * 1024


def _cparams(*sem):
    return pltpu.CompilerParams(dimension_semantics=sem, vmem_limit_bytes=VMEM_LIMIT)


def _sigmoid(x):
    return 1.0 / (1.0 + jnp.exp(-x))


def _silu(x):
    return x * _sigmoid(x)


def _gelu(x):
    return 0.5 * x * (1.0 + jnp.tanh(math.sqrt(2.0 / math.pi) * (x + 0.044715 * (x * x * x))))


def _log_sigmoid(x):
    return jnp.minimum(x, 0.0) - jnp.log(1.0 + jnp.exp(-jnp.abs(x)))


def _rms_mod(x, g, shift, scale):
    y = x * lax.rsqrt(jnp.mean(x * x, axis=-1, keepdims=True) + EPS) * g
    return y * (1.0 + scale) + shift


def _layernorm(x, g, b):
    mu = jnp.mean(x, axis=-1, keepdims=True)
    xc = x - mu
    return xc * lax.rsqrt(jnp.mean(xc * xc, axis=-1, keepdims=True) + EPS) * g + b


def _dot(a, b):
    return jnp.dot(a, b, preferred_element_type=F32)


def _dot_nt(a, b):
    return lax.dot_general(a, b, (((1,), (1,)), ((), ())), preferred_element_type=F32)


def _dot_exact(a, b):
    return jnp.dot(a, b, preferred_element_type=F32, precision=HIGHEST)


def _ada_kernel(s_ref, w_ref, b_ref, o_ref):
    s = _silu(s_ref[...])
    o_ref[0] = _dot_exact(s, w_ref[0]) + b_ref[0]


def _ada_call(cond, w_ada, b_ada):
    depth, d, n6 = w_ada.shape
    rows = cond.shape[0]
    tn = 1536
    return pl.pallas_call(
        _ada_kernel,
        out_shape=jax.ShapeDtypeStruct((depth, rows, n6), F32),
        grid=(depth, n6 // tn),
        in_specs=[pl.BlockSpec((rows, d), lambda l, j: (0, 0)),
                  pl.BlockSpec((1, d, tn), lambda l, j: (l, 0, j)),
                  pl.BlockSpec((1, 1, tn), lambda l, j: (l, 0, j))],
        out_specs=pl.BlockSpec((1, rows, tn), lambda l, j: (l, 0, j)),
        compiler_params=_cparams("arbitrary", "arbitrary"),
        name="ada_mod",
    )(cond, w_ada, b_ada.reshape(depth, 1, n6))


def _addpos_kernel(x_ref, p_ref, o_ref):
    o_ref[0] = x_ref[0] + p_ref[...]


def _addpos_call(x, pos):
    b, t, d = x.shape
    tm = min(t, 512)
    return pl.pallas_call(
        _addpos_kernel,
        out_shape=jax.ShapeDtypeStruct(x.shape, F32),
        grid=(t // tm, b),
        in_specs=[pl.BlockSpec((1, tm, d), lambda i, j: (j, i, 0)),
                  pl.BlockSpec((tm, d), lambda i, j: (i, 0))],
        out_specs=pl.BlockSpec((1, tm, d), lambda i, j: (j, i, 0)),
        compiler_params=_cparams("arbitrary", "arbitrary"),
        name="add_pos",
    )(x, pos)


def _mod_spec(row_fn, k):
    return pl.BlockSpec((1, 1, 1, D_MODEL), lambda *g: (row_fn(*g), k, 0, 0))


def _const_spec(shape):
    nd = len(shape)
    return pl.BlockSpec(shape, lambda *g: (0,) * nd)


def _inproj_ml_kernel(x_ref, sh_ref, sc_ref, g_ref, w_ref, wgc_ref, wgr_ref, p_ref, gc_ref, gr_ref):
    h = _rms_mod(x_ref[0], g_ref[...], sh_ref[0, 0], sc_ref[0, 0]).astype(BF16)
    p_ref[0] = _dot(h, w_ref[...])
    gc_ref[0] = _dot(h, wgc_ref[...])
    gr_ref[0] = _dot_nt(wgr_ref[...], h)


def _inproj_ml_call(x, mod, row_fn, norm_g, w_ml, w_gc, w_gr):
    b, t, d = x.shape
    tm = 256
    n_ml = w_ml.shape[1]
    return pl.pallas_call(
        _inproj_ml_kernel,
        out_shape=(jax.ShapeDtypeStruct((b, t, n_ml), F32),
                   jax.ShapeDtypeStruct((b, t, LANE), F32),
                   jax.ShapeDtypeStruct((b, 16, t), F32)),
        grid=(b, t // tm),
        in_specs=[pl.BlockSpec((1, tm, d), lambda i, j: (i, j, 0)),
                  _mod_spec(lambda i, j: row_fn(i), 0),
                  _mod_spec(lambda i, j: row_fn(i), 1),
                  _const_spec((1, d)),
                  _const_spec(w_ml.shape),
                  _const_spec(w_gc.shape),
                  _const_spec(w_gr.shape)],
        out_specs=(pl.BlockSpec((1, tm, n_ml), lambda i, j: (i, j, 0)),
                   pl.BlockSpec((1, tm, LANE), lambda i, j: (i, j, 0)),
                   pl.BlockSpec((1, 16, tm), lambda i, j: (i, 0, j))),
        compiler_params=_cparams("arbitrary", "arbitrary"),
        name="inproj_ml",
    )(x, mod, mod, norm_g, w_ml, w_gc, w_gr)


def _inproj_rest_kernel(x_ref, sh_ref, sc_ref, g_ref, w_ref, lng_ref, lnb_ref, wcat_ref, bgm_ref, cs_ref,
                        zgm_ref, zglu_ref, fa_ref, fb_ref, gate_ref):
    tm = x_ref.shape[1]
    h = _rms_mod(x_ref[0], g_ref[...], sh_ref[0, 0], sc_ref[0, 0]).astype(BF16)
    z = _gelu(_dot(h, w_ref[:, 0:2 * MIX_W]))
    u = z[:, :MIX_W]
    v = _layernorm(z[:, MIX_W:], lng_ref[...], lnb_ref[...])
    grp = lax.broadcasted_iota(jnp.int32, (GM_CHUNK, MIX_W), 1) // GM_DG
    for j in range(tm // GM_CHUNK):
        vc = v[j * GM_CHUNK:(j + 1) * GM_CHUNK]
        vstack = jnp.concatenate([jnp.where(grp == g, vc, 0.0) for g in range(GM_GROUPS)], axis=0).astype(BF16)
        s = _dot(wcat_ref[...], vstack) + bgm_ref[...]
        zgm_ref[0, j * GM_CHUNK:(j + 1) * GM_CHUNK, :] = u[j * GM_CHUNK:(j + 1) * GM_CHUNK] * s
    pc = _dot(h, w_ref[:, 2 * MIX_W:4 * MIX_W])
    zglu_ref[0] = pc[:, :MIX_W] * _sigmoid(pc[:, MIX_W:])
    pf = _dot(h, w_ref[:, 4 * MIX_W:5 * MIX_W]).astype(BF16)
    ab = _dot(pf, cs_ref[...])
    fa_ref[...] = ab[:, :MIX_W].astype(BF16)
    fb_ref[...] = ab[:, MIX_W:].astype(BF16)
    for i in range(4):
        lo = 5 * MIX_W + i * D_MODEL
        gate_ref[0, :, i * D_MODEL:(i + 1) * D_MODEL] = _sigmoid(_dot(h, w_ref[:, lo:lo + D_MODEL])).astype(BF16)


def _inproj_rest_call(x, mod, row_fn, norm_g, w_rest, gm_ln_g, gm_ln_b, wcat, bias_gm, cs):
    b, t, d = x.shape
    tm = 256
    return pl.pallas_call(
        _inproj_rest_kernel,
        out_shape=(jax.ShapeDtypeStruct((b, t, MIX_W), F32),
                   jax.ShapeDtypeStruct((b, t, MIX_W), F32),
                   jax.ShapeDtypeStruct((t, b * MIX_W), BF16),
                   jax.ShapeDtypeStruct((t, b * MIX_W), BF16),
                   jax.ShapeDtypeStruct((b, t, 4 * D_MODEL), BF16)),
        grid=(b, t // tm),
        in_specs=[pl.BlockSpec((1, tm, d), lambda i, j: (i, j, 0)),
                  _mod_spec(lambda i, j: row_fn(i), 0),
                  _mod_spec(lambda i, j: row_fn(i), 1),
                  _const_spec((1, d)),
                  _const_spec(w_rest.shape),
                  _const_spec((1, MIX_W)), _const_spec((1, MIX_W)),
                  _const_spec(wcat.shape), _const_spec(bias_gm.shape), _const_spec(cs.shape)],
        out_specs=(pl.BlockSpec((1, tm, MIX_W), lambda i, j: (i, j, 0)),
                   pl.BlockSpec((1, tm, MIX_W), lambda i, j: (i, j, 0)),
                   pl.BlockSpec((tm, MIX_W), lambda i, j: (j, i)),
                   pl.BlockSpec((tm, MIX_W), lambda i, j: (j, i)),
                   pl.BlockSpec((1, tm, 4 * D_MODEL), lambda i, j: (i, j, 0))),
        compiler_params=_cparams("arbitrary", "arbitrary"),
        name="inproj_rest",
    )(x, mod, mod, norm_g, w_rest, gm_ln_g, gm_ln_b, wcat, bias_gm, cs)


def _dwconv_kernel(x_ref, prev_ref, next_ref, w_ref, b_ref, o_ref, win_ref, *, taps, act):
    i = pl.program_id(2)
    last = pl.num_programs(2) - 1
    rc = x_ref.shape[1]
    win_ref[0:CONV_HALO, :] = jnp.where(i > 0, prev_ref[0], 0.0)
    win_ref[CONV_HALO:CONV_HALO + rc, :] = x_ref[0]
    win_ref[CONV_HALO + rc:, :] = jnp.where(i < last, next_ref[0], 0.0)
    sub = 128
    for r in range(rc // sub):
        acc = jnp.broadcast_to(b_ref[...], (sub, LANE))
        for j in range(taps):
            lo = CONV_HALO + r * sub + j - taps // 2
            acc = acc + w_ref[j:j + 1, :] * win_ref[lo:lo + sub, :]
        if act:
            acc = _silu(acc)
        o_ref[0, r * sub:(r + 1) * sub, :] = acc


def _dwconv_call(x, col0, ncol, w, bias, act):
    b, t, _ = x.shape
    taps = w.shape[0]
    rc = min(t, 512)
    c0 = col0 // LANE
    hb = rc // CONV_HALO
    nh = t // CONV_HALO
    return pl.pallas_call(
        functools.partial(_dwconv_kernel, taps=taps, act=act),
        out_shape=jax.ShapeDtypeStruct((b, t, ncol), F32),
        grid=(b, ncol // LANE, t // rc),
        in_specs=[pl.BlockSpec((1, rc, LANE), lambda bi, c, i: (bi, i, c0 + c)),
                  pl.BlockSpec((1, CONV_HALO, LANE), lambda bi, c, i: (bi, jnp.maximum(i * hb - 1, 0), c0 + c)),
                  pl.BlockSpec((1, CONV_HALO, LANE), lambda bi, c, i: (bi, jnp.minimum((i + 1) * hb, nh - 1), c0 + c)),
                  pl.BlockSpec((taps, LANE), lambda bi, c, i: (0, c)),
                  pl.BlockSpec((1, LANE), lambda bi, c, i: (0, c))],
        out_specs=pl.BlockSpec((1, rc, LANE), lambda bi, c, i: (bi, i, c)),
        scratch_shapes=[pltpu.VMEM((rc + 2 * CONV_HALO, LANE), F32)],
        compiler_params=_cparams("arbitrary", "arbitrary", "arbitrary"),
        name="dwconv_k%d" % taps,
    )(x, x, x, w, bias)


def _mlstm_kernel(*refs, need_out):
    (qkf_ref, qkb_ref, vf_ref, vb_ref, gcf_ref, gcb_ref, grf_ref, grb_ref, gbc_ref, gbr_ref,
     cn_in_ref, m_in_ref) = refs[:12]
    if need_out:
        hf_ref, hb_ref, cn_out_ref, m_out_ref, cn_s, m_s = refs[12:]
    else:
        cn_out_ref, m_out_ref, cn_s, m_s = refs[12:]
        hf_ref = hb_ref = None
    j = pl.program_id(1)
    last = pl.num_programs(1) - 1
    L = ML_CHUNK

    @pl.when(j == 0)
    def _():
        cn_s[...] = cn_in_ref[0]
        m_s[...] = m_in_ref[0]

    row = lax.broadcasted_iota(jnp.int32, (L, L), 0)
    col = lax.broadcasted_iota(jnp.int32, (L, L), 1)
    lane = lax.broadcasted_iota(jnp.int32, (1, HEAD_PAD), 1)
    one_at_dh = (lane == ML_DH).astype(F32)
    keep_dh = (lane < ML_DH).astype(F32)

    for direction in range(2):
        qk_ref, v_ref, gc_ref, gr_ref, h_ref = ((qkf_ref, vf_ref, gcf_ref, grf_ref, hf_ref) if direction == 0
                                                else (qkb_ref, vb_ref, gcb_ref, grb_ref, hb_ref))
        visible = (col <= row) if direction == 0 else (col >= row)
        tri_c = visible.astype(F32)
        tri_r = ((row <= col) if direction == 0 else (row >= col)).astype(F32)
        gcv = gc_ref[0] + gbc_ref[...]
        grv = gr_ref[0] + gbr_ref[...]
        lf_c = _log_sigmoid(gcv)
        lf_r = _log_sigmoid(grv)
        cum_c = _dot_exact(tri_c, lf_c)
        cum_r = _dot_exact(lf_r, tri_r)
        tot_c = jnp.sum(lf_c, axis=0, keepdims=True)
        for hd in range(ML_HEADS):
            g = direction * ML_HEADS + hd
            gi_col = g
            gf_col = 2 * ML_HEADS + g
            li_r = grv[gi_col:gi_col + 1, :]
            li_c = gcv[:, gi_col:gi_col + 1]
            cc = cum_c[:, gf_col:gf_col + 1]
            cr = cum_r[gf_col:gf_col + 1, :]
            tot = tot_c[:, gf_col:gf_col + 1]
            m_prev = m_s[g][:, 0:1]
            cn = cn_s[g]
            qh = qk_ref[0, :, hd * HEAD_PAD:(hd + 1) * HEAD_PAD]
            kh = qk_ref[0, :, ML_PAD_W + hd * HEAD_PAD:ML_PAD_W + (hd + 1) * HEAD_PAD] * (ML_DH ** -0.5)
            v1 = v_ref[0, :, hd * HEAD_PAD:(hd + 1) * HEAD_PAD] + one_at_dh
            w_r = tot - cr + li_r
            w_c = tot - cc + li_c
            m_new = jnp.maximum(tot + m_prev, jnp.max(w_r, axis=1, keepdims=True))
            ws_c = jnp.exp(w_c - m_new)
            decay = jnp.exp(tot + m_prev - m_new)
            if need_out:
                dlog = jnp.where(visible, cc - cr + li_r, -jnp.inf)
                inter = cc + m_prev
                m_t = jnp.maximum(inter, jnp.max(dlog, axis=1, keepdims=True))
                s = _dot_nt(qh.astype(BF16), kh.astype(BF16)) * jnp.exp(dlog - m_t)
                gi = jnp.exp(inter - m_t)
                nd = gi * _dot(qh.astype(BF16), cn.astype(BF16)) + _dot(s.astype(BF16), v1.astype(BF16))
                den = nd[:, ML_DH:ML_DH + 1]
                hh = nd / jnp.maximum(jnp.abs(den), jnp.exp(-m_t))
                h_ref[0, :, hd * HEAD_PAD:(hd + 1) * HEAD_PAD] = hh * keep_dh
            kw = (kh * ws_c).astype(BF16)
            upd = lax.dot_general(kw, v1.astype(BF16), (((0,), (0,)), ((), ())), preferred_element_type=F32)
            cn_s[g] = decay * cn + upd
            m_s[g] = jnp.broadcast_to(m_new, (1, LANE))

    @pl.when(j == last)
    def _():
        cn_out_ref[0] = cn_s[...]
        m_out_ref[0] = m_s[...]


def _mlstm_call(qk, p_ml, gc, gr, gbc, gbr, cn_in, m_in, need_out):
    b, t, _ = qk.shape
    L = ML_CHUNK
    nc = t // L
    g2 = 2 * ML_HEADS
    fwd = lambda i, j: (i, j, 0)
    bwd = lambda i, j: (i, nc - 1 - j, 0)
    in_specs = [pl.BlockSpec((1, L, 2 * ML_PAD_W), fwd),
                pl.BlockSpec((1, L, 2 * ML_PAD_W), bwd),
                pl.BlockSpec((1, L, ML_PAD_W), lambda i, j: (i, j, 2)),
                pl.BlockSpec((1, L, ML_PAD_W), lambda i, j: (i, nc - 1 - j, 2)),
                pl.BlockSpec((1, L, LANE), fwd),
                pl.BlockSpec((1, L, LANE), bwd),
                pl.BlockSpec((1, 16, L), lambda i, j: (i, 0, j)),
                pl.BlockSpec((1, 16, L), lambda i, j: (i, 0, nc - 1 - j)),
                _const_spec((1, LANE)),
                _const_spec((16, L)),
                pl.BlockSpec((1, g2, HEAD_PAD, HEAD_PAD), lambda i, j: (i, 0, 0, 0)),
                pl.BlockSpec((1, g2, 1, LANE), lambda i, j: (i, 0, 0, 0))]
    state_shapes = (jax.ShapeDtypeStruct((b, g2, HEAD_PAD, HEAD_PAD), F32),
                    jax.ShapeDtypeStruct((b, g2, 1, LANE), F32))
    state_specs = (pl.BlockSpec((1, g2, HEAD_PAD, HEAD_PAD), lambda i, j: (i, 0, 0, 0)),
                   pl.BlockSpec((1, g2, 1, LANE), lambda i, j: (i, 0, 0, 0)))
    if need_out:
        out_shape = (jax.ShapeDtypeStruct((b, t, ML_PAD_W), F32),
                     jax.ShapeDtypeStruct((b, t, ML_PAD_W), F32)) + state_shapes
        out_specs = (pl.BlockSpec((1, L, ML_PAD_W), fwd), pl.BlockSpec((1, L, ML_PAD_W), bwd)) + state_specs
    else:
        out_shape, out_specs = state_shapes, state_specs
    return pl.pallas_call(
        functools.partial(_mlstm_kernel, need_out=need_out),
        out_shape=out_shape,
        grid=(b, nc),
        in_specs=in_specs,
        out_specs=out_specs,
        scratch_shapes=[pltpu.VMEM((g2, HEAD_PAD, HEAD_PAD), F32), pltpu.VMEM((g2, 1, LANE), F32)],
        compiler_params=_cparams("arbitrary", "arbitrary"),
        name="mlstm_out" if need_out else "mlstm_state",
    )(qk, qk, p_ml, p_ml, gc, gc, gr, gr, gbc, gbr, cn_in, m_in)


def _fourier_kernel(ct_ref, st_ref, a_ref, b_ref, o_ref):
    @pl.when(pl.program_id(1) == 0)
    def _():
        o_ref[...] = jnp.zeros_like(o_ref)
    o_ref[...] += _dot(ct_ref[...], a_ref[...]) + _dot(st_ref[...], b_ref[...])


def _fourier_call(ct, mst, fa, fb):
    t, n = fa.shape
    tm = min(t, 512)
    return pl.pallas_call(
        _fourier_kernel,
        out_shape=jax.ShapeDtypeStruct((t, n), F32),
        grid=(t // tm, t // tm),
        in_specs=[pl.BlockSpec((tm, tm), lambda i, k: (i, k)),
                  pl.BlockSpec((tm, tm), lambda i, k: (i, k)),
                  pl.BlockSpec((tm, n), lambda i, k: (k, 0)),
                  pl.BlockSpec((tm, n), lambda i, k: (k, 0))],
        out_specs=pl.BlockSpec((tm, n), lambda i, k: (i, 0)),
        compiler_params=_cparams("arbitrary", "arbitrary"),
        name="fourier_pos",
    )(ct, mst, fa, fb)


def _merge_kernel(x_ref, m2_ref, m3_ref, m4_ref, hf_ref, hb_ref, o_ref, zgm_ref, cv_ref, zft_ref, gate_ref,
                  mlg_ref, cvg_ref, cvb_ref, n2g_ref, wb0_ref, wb1_ref, wb2_ref, wb3_ref, wout_ref,
                  xo_ref, h2_ref):
    hsum = hf_ref[0] + hb_ref[0]
    parts = []
    for hd in range(ML_HEADS):
        blk = hsum[:, hd * HEAD_PAD:(hd + 1) * HEAD_PAD]
        ms = jnp.sum(blk * blk, axis=-1, keepdims=True) * (1.0 / ML_DH)
        parts.append(blk * lax.rsqrt(ms + EPS))
    z_ml = jnp.concatenate(parts, axis=1) * mlg_ref[...] * _sigmoid(o_ref[0])
    z_cv = _silu(_layernorm(cv_ref[0], cvg_ref[...], cvb_ref[...]))

    def gate(i):
        return gate_ref[0, :, i * D_MODEL:(i + 1) * D_MODEL].astype(F32)

    y = gate(0) * _dot(z_ml.astype(BF16), wb0_ref[...])
    y = y + gate(1) * _dot(zgm_ref[0].astype(BF16), wb1_ref[...])
    y = y + gate(2) * _dot(z_cv.astype(BF16), wb2_ref[...])
    y = y + gate(3) * _dot(zft_ref[...].astype(BF16), wb3_ref[...])
    xo = x_ref[0] + m2_ref[0, 0] * _dot(y.astype(BF16), wout_ref[...])
    xo_ref[0] = xo
    h2_ref[0] = _rms_mod(xo, n2g_ref[...], m3_ref[0, 0], m4_ref[0, 0])


def _merge_call(x, mod, row_fn, hf, hb, p_ml, zgm, cv, zft, gate, mlg, cvg, cvb, n2g, wb0, wb1, wb2, wb3, wout):
    b, t, d = x.shape
    tm = 256
    tok = lambda w: pl.BlockSpec((1, tm, w), lambda i, j: (i, j, 0))
    rf = lambda i, j: row_fn(i)
    return pl.pallas_call(
        _merge_kernel,
        out_shape=(jax.ShapeDtypeStruct((b, t, d), F32), jax.ShapeDtypeStruct((b, t, d), F32)),
        grid=(b, t // tm),
        in_specs=[tok(d), _mod_spec(rf, 2), _mod_spec(rf, 3), _mod_spec(rf, 4),
                  tok(ML_PAD_W), tok(ML_PAD_W),
                  pl.BlockSpec((1, tm, ML_PAD_W), lambda i, j: (i, j, 3)),
                  tok(MIX_W), tok(MIX_W),
                  pl.BlockSpec((tm, MIX_W), lambda i, j: (j, i)),
                  tok(4 * d),
                  _const_spec((1, ML_PAD_W)), _const_spec((1, MIX_W)), _const_spec((1, MIX_W)), _const_spec((1, d)),
                  _const_spec(wb0.shape), _const_spec(wb1.shape), _const_spec(wb2.shape), _const_spec(wb3.shape),
                  _const_spec(wout.shape)],
        out_specs=(tok(d), tok(d)),
        compiler_params=_cparams("arbitrary", "arbitrary"),
        name="merge",
    )(x, mod, mod, mod, hf, hb, p_ml, zgm, cv, zft, gate, mlg, cvg, cvb, n2g, wb0, wb1, wb2, wb3, wout)


def _top16_rows(s, ids):
    big = jnp.float32(2 ** 30)
    out_rows = lax.broadcasted_iota(jnp.int32, (PEER_TOPK, LANE), 0)
    vals = jnp.zeros((PEER_TOPK, LANE), F32)
    sel = jnp.zeros((PEER_TOPK, LANE), F32)
    for i in range(PEER_TOPK):
        m = jnp.max(s, axis=0, keepdims=True)
        pick = jnp.min(jnp.where(s == m, ids, big), axis=0, keepdims=True)
        s = jnp.where(ids == pick, -jnp.inf, s)
        vals = jnp.where(out_rows == i, m, vals)
        sel = jnp.where(out_rows == i, pick, sel)
    return vals, sel


def _peer_stage2(v1, i1, v2, i2):
    big = jnp.float32(2 ** 30)
    sub_i = lax.broadcasted_iota(jnp.int32, (8, LANE), 0)
    sub = sub_i.astype(F32)
    tiles = []
    for a in range(8):
        va = v1[a:a + 1, :]
        ea = i1[a:a + 1, :] * float(N_KEYS)
        for half in range(2 if a == 0 else 1):
            ok = (a + 1) * (sub_i + 8 * half + 1) <= PEER_TOPK
            sc = jnp.where(ok, va + v2[8 * half:8 * half + 8, :], -jnp.inf)
            tiles.append((sc, sub + float(a * PEER_TOPK + 8 * half), ea + i2[8 * half:8 * half + 8, :]))
    tiles.append((v1[8:16, :] + v2[0:1, :], (sub + 8.0) * float(PEER_TOPK),
                  i1[8:16, :] * float(N_KEYS) + i2[0:1, :]))
    sc = jnp.concatenate([t[0] for t in tiles], axis=0)
    pos = jnp.concatenate([t[1] for t in tiles], axis=0)
    eid = jnp.concatenate([t[2] for t in tiles], axis=0)
    out_rows = lax.broadcasted_iota(jnp.int32, (PEER_TOPK, LANE), 0)
    vals = jnp.zeros((PEER_TOPK, LANE), F32)
    sel = jnp.zeros((PEER_TOPK, LANE), F32)
    for i in range(PEER_TOPK):
        m = jnp.max(sc, axis=0, keepdims=True)
        pick = jnp.min(jnp.where(sc == m, pos, big), axis=0, keepdims=True)
        hit = pos == pick
        e = jnp.max(jnp.where(hit, eid, -1.0), axis=0, keepdims=True)
        sc = jnp.where(hit, -jnp.inf, sc)
        vals = jnp.where(out_rows == i, m, vals)
        sel = jnp.where(out_rows == i, e, sel)
    return vals, sel


def _route_kernel(h_ref, wq_ref, keys_ref, e_ref, g_ref):
    tm = h_ref.shape[0]
    q = _dot(h_ref[...].astype(BF16), wq_ref[...])
    key_ids = lax.broadcasted_iota(jnp.int32, (N_KEYS, LANE), 0).astype(F32)
    for c in range(tm // LANE):
        ids, ws = [], []
        for hd in range(PEER_HEADS):
            tops = []
            for p in range(2):
                lo = (hd * 2 + p) * N_KEYS
                qs = q[c * LANE:(c + 1) * LANE, lo:lo + N_KEYS].astype(BF16)
                s = _dot_nt(keys_ref[p], qs)
                tops.append(_top16_rows(s, key_ids))
            sc, eid = _peer_stage2(tops[0][0], tops[0][1], tops[1][0], tops[1][1])
            w = jnp.exp(sc - sc[0:1, :])
            ids.append(eid)
            ws.append(w / jnp.sum(w, axis=0, keepdims=True))
        e_ref[c * LANE:(c + 1) * LANE, :] = jnp.concatenate(ids, axis=0).T.astype(jnp.int32)
        g_ref[c * LANE:(c + 1) * LANE, :] = jnp.concatenate(ws, axis=0).T


def _route_call(h2, wq, keys):
    n, d = h2.shape
    tm = 256
    return pl.pallas_call(
        _route_kernel,
        out_shape=(jax.ShapeDtypeStruct((n, PEER_SEL), jnp.int32), jax.ShapeDtypeStruct((n, PEER_SEL), F32)),
        grid=(n // tm,),
        in_specs=[pl.BlockSpec((tm, d), lambda i: (i, 0)), _const_spec(wq.shape), _const_spec(keys.shape)],
        out_specs=(pl.BlockSpec((tm, PEER_SEL), lambda i: (i, 0)), pl.BlockSpec((tm, PEER_SEL), lambda i: (i, 0))),
        compiler_params=_cparams("arbitrary"),
        name="peer_route",
    )(h2, wq, keys)


PEER_TB = 64
PEER_NBUF = 4
ROW_TILES = 2 * D_MODEL // LANE


def _experts_kernel(idx_ref, g_ref, h_ref, x_ref, m5_ref, fg_ref, uv_ref, fill_ref, o_ref, buf, sem, *, final):
    tb = h_ref.shape[0]
    nt = D_MODEL // LANE

    def gather(t, slot):
        for r in range(PEER_SEL):
            pltpu.make_async_copy(uv_ref.at[idx_ref[t, r]], buf.at[slot, :, pl.ds(r, 1), :], sem.at[slot]).start()

    def wait(slot):
        pltpu.make_async_copy(fill_ref, buf.at[slot], sem.at[slot]).wait()

    eye = (lax.broadcasted_iota(jnp.int32, (PEER_SEL, PEER_SEL), 0)
           == lax.broadcasted_iota(jnp.int32, (PEER_SEL, PEER_SEL), 1))

    def compute(t, slot):
        hrow = h_ref[pl.ds(t, 1), :]
        acc = buf[slot, 0] * hrow[:, 0:LANE]
        for c in range(1, nt):
            acc = acc + buf[slot, c] * hrow[:, c * LANE:(c + 1) * LANE]
        s = jnp.sum(acc, axis=1, keepdims=True)
        gcol = jnp.sum(jnp.where(eye, g_ref[pl.ds(t, 1), :], 0.0), axis=1, keepdims=True)
        a = _gelu(s) * gcol
        o = jnp.concatenate([jnp.sum(buf[slot, nt + c] * a, axis=0, keepdims=True) for c in range(nt)], axis=1)
        xo = x_ref[pl.ds(t, 1), :] + m5_ref[0, 0] * o
        if final:
            xo = xo * lax.rsqrt(jnp.mean(xo * xo, axis=-1, keepdims=True) + EPS) * fg_ref[...]
        o_ref[pl.ds(t, 1), :] = xo

    for slot in range(PEER_NBUF):
        gather(slot, slot)

    def body(i, carry):
        for slot in range(PEER_NBUF):
            t = i * PEER_NBUF + slot
            wait(slot)
            compute(t, slot)

            @pl.when(t + PEER_NBUF < tb)
            def _():
                gather(t + PEER_NBUF, slot)
        return carry

    lax.fori_loop(0, tb // PEER_NBUF, body, 0)


def _experts_call(idx, g, h2, x, mod, row_of_block, final_g, uv, final, n_tc):
    n, d = h2.shape
    tb = PEER_TB
    tok = lambda w: pl.BlockSpec((tb, w), lambda i: (i, 0))
    fill = jnp.zeros((ROW_TILES, PEER_SEL, LANE), F32)
    return pl.pallas_call(
        functools.partial(_experts_kernel, final=final),
        out_shape=jax.ShapeDtypeStruct((n, d), F32),
        grid=(n_tc // tb,),
        in_specs=[pl.BlockSpec((tb, PEER_SEL), lambda i: (i, 0), memory_space=pltpu.SMEM),
                  tok(PEER_SEL), tok(d), tok(d),
                  _mod_spec(lambda i: row_of_block(i), 5),
                  _const_spec((1, d)),
                  pl.BlockSpec(memory_space=pl.ANY),
                  pl.BlockSpec(memory_space=pl.ANY)],
        out_specs=tok(d),
        scratch_shapes=[pltpu.VMEM((PEER_NBUF, ROW_TILES, PEER_SEL, LANE), F32),
                        pltpu.SemaphoreType.DMA((PEER_NBUF,))],
        compiler_params=_cparams("arbitrary"),
        name="peer_experts_final" if final else "peer_experts",
    )(idx, g, h2, x, mod, final_g, uv, fill)


SC_CORES = 2
SC_SUBCORES = 16
SC_LANES = 16
SC_WORKERS = SC_CORES * SC_SUBCORES
SC_ROWS = 16
SC_TBATCH = 8
SC_GROUPS = 4
SC_TOKENS = 5376


def _sc_gelu(x):
    y = math.sqrt(2.0 / math.pi) * (x + 0.044715 * (x * x * x))
    tanh_y = 1.0 - 2.0 / (jnp.exp(2.0 * y) + 1.0)
    return 0.5 * x * (1.0 + tanh_y)


def _sc_experts_call(idx, g, h2, uv2d, ns):
    n, d = h2.shape
    tok0 = n - ns
    per = ns // SC_WORKERS
    nv = d // SC_LANES
    mesh = plsc.VectorSubcoreMesh(core_axis_name="c", subcore_axis_name="s")

    nch = PEER_SEL // SC_ROWS
    nq = SC_TBATCH * nch
    assert SC_ROWS == SC_LANES and nch & (nch - 1) == 0 and per % SC_TBATCH == 0
    shift = nch.bit_length() - 1

    def body(idx_hbm, g_hbm, h_hbm, uv_hbm, out_hbm, idx_v, g_v, h_v, rows_a, rows_b, out_v, sem_a, sem_b):
        wid = lax.axis_index("s") * SC_CORES + lax.axis_index("c")
        base = wid * per
        lane = lax.iota(jnp.int32, SC_LANES)
        zero = jnp.zeros((SC_LANES,), F32)

        def gather(q, rows, sem):
            tl = lax.shift_right_logical(q, shift)
            ch = jnp.bitwise_and(q, nch - 1)
            ids = idx_v[tl, pl.ds(ch * SC_ROWS, SC_ROWS)]
            return pltpu.make_async_copy(uv_hbm.at[ids], rows, sem)

        def compute(q, rows):
            tl = lax.shift_right_logical(q, shift)
            ch = jnp.bitwise_and(q, nch - 1)

            def ubody(j, accs):
                hj = h_v[tl, pl.ds(j * SC_LANES, SC_LANES)]
                return tuple(accs[i] + rows[i, pl.ds(j * SC_LANES, SC_LANES)] * hj for i in range(SC_LANES))
            accs = lax.fori_loop(0, nv, ubody, tuple(zero for _ in range(SC_LANES)))
            s = zero
            for i in range(SC_LANES):
                s = jnp.where(lane == i, jnp.sum(accs[i]), s)
            a = _sc_gelu(s) * g_v[tl, pl.ds(ch * SC_ROWS, SC_LANES)]
            ab = [jnp.sum(jnp.where(lane == i, a, 0.0)) for i in range(SC_LANES)]

            def vbody(cb, c):
                col = cb * (SC_LANES * SC_LANES)
                acc = [out_v[tl, pl.ds(col + k * SC_LANES, SC_LANES)] for k in range(SC_LANES)]
                for i in range(SC_LANES):
                    for k in range(SC_LANES):
                        acc[k] = acc[k] + rows[i, pl.ds(d + col + k * SC_LANES, SC_LANES)] * ab[i]
                for k in range(SC_LANES):
                    out_v[tl, pl.ds(col + k * SC_LANES, SC_LANES)] = acc[k]
                return c
            lax.fori_loop(0, d // (SC_LANES * SC_LANES), vbody, 0)

        def batch(bi, carry):
            t0 = base + bi * SC_TBATCH
            pltpu.sync_copy(idx_hbm.at[pl.ds(tok0 + t0, SC_TBATCH)], idx_v)
            pltpu.sync_copy(g_hbm.at[pl.ds(tok0 + t0, SC_TBATCH)], g_v)
            pltpu.sync_copy(h_hbm.at[pl.ds(tok0 + t0, SC_TBATCH)], h_v)

            def clear(j, c):
                for tl in range(SC_TBATCH):
                    out_v[tl, pl.ds(j * SC_LANES, SC_LANES)] = zero
                return c
            lax.fori_loop(0, nv, clear, 0)

            gather(0, rows_a, sem_a).start()

            def pair(p, c):
                q = 2 * p
                gather(q + 1, rows_b, sem_b).start()
                gather(q, rows_a, sem_a).wait()
                compute(q, rows_a)

                @pl.when(q + 2 < nq)
                def _():
                    gather(q + 2, rows_a, sem_a).start()
                gather(q + 1, rows_b, sem_b).wait()
                compute(q + 1, rows_b)
                return c
            lax.fori_loop(0, nq // 2, pair, 0)
            pltpu.sync_copy(out_v, out_hbm.at[pl.ds(t0, SC_TBATCH)])
            return carry

        lax.fori_loop(0, per // SC_TBATCH, batch, 0)

    return pl.kernel(
        body,
        out_type=jax.ShapeDtypeStruct((ns, d), F32),
        mesh=mesh,
        scratch_types=[pltpu.VMEM((SC_TBATCH, PEER_SEL), jnp.int32), pltpu.VMEM((SC_TBATCH, PEER_SEL), F32),
                       pltpu.VMEM((SC_TBATCH, d), F32),
                       pltpu.VMEM((SC_ROWS, 2 * d), F32), pltpu.VMEM((SC_ROWS, 2 * d), F32),
                       pltpu.VMEM((SC_TBATCH, d), F32),
                       pltpu.SemaphoreType.DMA, pltpu.SemaphoreType.DMA],
        compiler_params=pltpu.CompilerParams(needs_layout_passes=False),
        name="peer_experts_sc",
    )(idx, g, h2, uv2d)


def _residual_kernel(x_ref, p_ref, m5_ref, fg_ref, prev_ref, o_ref, *, final):
    del prev_ref
    xo = x_ref[...] + m5_ref[0, 0] * p_ref[...]
    if final:
        xo = xo * lax.rsqrt(jnp.mean(xo * xo, axis=-1, keepdims=True) + EPS) * fg_ref[...]
    o_ref[...] = xo


def _residual_call(x, p, prev, mod, row_of_block, final_g, final):
    n, d = x.shape
    ns = p.shape[0]
    tb = PEER_TB
    first = (n - ns) // tb
    tail = pl.BlockSpec((tb, d), lambda i: (i + first, 0))
    return pl.pallas_call(
        functools.partial(_residual_kernel, final=final),
        out_shape=jax.ShapeDtypeStruct((n, d), F32),
        grid=(ns // tb,),
        in_specs=[tail, pl.BlockSpec((tb, d), lambda i: (i, 0)),
                  _mod_spec(lambda i: row_of_block(i + first), 5), _const_spec((1, d)),
                  pl.BlockSpec(memory_space=pl.ANY)],
        out_specs=tail,
        input_output_aliases={4: 0},
        compiler_params=_cparams("arbitrary"),
        name="peer_residual",
    )(x, p, mod, final_g, prev)


def _grid_sincos(n_tok, d):
    rows = n_tok // GRID_W
    n_freq = d // 4
    freq = 1.0 / (10000.0 ** (jnp.arange(n_freq, dtype=F32) / n_freq))
    r = jnp.repeat(jnp.arange(rows, dtype=F32), GRID_W)
    cc = jnp.tile(jnp.arange(GRID_W, dtype=F32), rows)
    ar = r[:, None] * freq[None, :]
    ac = cc[:, None] * freq[None, :]
    return jnp.concatenate([jnp.sin(ar), jnp.cos(ar), jnp.sin(ac), jnp.cos(ac)], axis=-1)


def _dft_tables(t_len):
    i = jnp.arange(t_len, dtype=jnp.int32)
    ph = (i[:, None] * i[None, :]) % t_len
    ang = ph.astype(F32) * (2.0 * math.pi / t_len)
    scale = 1.0 / math.sqrt(t_len * FT_DG)
    return (jnp.cos(ang) * scale).astype(BF16), (-jnp.sin(ang) * scale).astype(BF16)


def _channel_dft():
    i = jnp.arange(MIX_W, dtype=jnp.int32)
    same = (i[:, None] // FT_DG) == (i[None, :] // FT_DG)
    ph = ((i[:, None] % FT_DG) * (i[None, :] % FT_DG)) % FT_DG
    ang = ph.astype(F32) * (2.0 * math.pi / FT_DG)
    c = jnp.where(same, jnp.cos(ang), 0.0)
    s = jnp.where(same, jnp.sin(ang), 0.0)
    return jnp.concatenate([c, s], axis=1).astype(BF16)


def _pad_heads(a):
    lead = a.shape[:-1]
    a = a.reshape(lead + (ML_HEADS, ML_DH))
    a = jnp.pad(a, [(0, 0)] * len(lead) + [(0, 0), (0, HEAD_PAD - ML_DH)])
    return a.reshape(lead + (ML_PAD_W,))


def _layer_weights(l, w_in, ml_conv_w, ml_conv_b, ml_gate_b, ml_norm_g, gm_w_s, gm_b_s, w_branch):
    w = w_in[l]
    q, k, v = w[:, 0:256], w[:, 256:512], w[:, 512:768]
    gates, o = w[:, 768:784], w[:, 784:1040]
    w_ml = jnp.concatenate([_pad_heads(q), _pad_heads(k), _pad_heads(v), _pad_heads(o)], axis=1).astype(BF16)
    w_gc = jnp.pad(gates, ((0, 0), (0, LANE - 16))).astype(BF16)
    w_gr = gates.T.astype(BF16)
    w_rest = w[:, 1040:].astype(BF16)
    cw = ml_conv_w[l]
    conv_w = jnp.concatenate([_pad_heads(cw[:, :256]), _pad_heads(cw[:, 256:])], axis=1)
    cb = ml_conv_b[l]
    conv_b = jnp.concatenate([_pad_heads(cb[:256]), _pad_heads(cb[256:])])[None, :]
    gb = ml_gate_b[l]
    gbc = jnp.pad(gb, (0, LANE - 16))[None, :]
    gbr = jnp.broadcast_to(gb[:, None], (16, ML_CHUNK))
    mlg = _pad_heads(ml_norm_g[l].reshape(-1))[None, :]
    wcat = jnp.transpose(gm_w_s[l], (1, 0, 2)).reshape(GM_CHUNK, GM_GROUPS * GM_CHUNK).astype(BF16)
    bias_gm = jnp.repeat(gm_b_s[l].T, GM_DG, axis=1)
    wb = w_branch[l]
    wb0 = jnp.pad(wb[0].reshape(ML_HEADS, ML_DH, D_MODEL), ((0, 0), (0, HEAD_PAD - ML_DH), (0, 0)))
    wb0 = wb0.reshape(ML_PAD_W, D_MODEL).astype(BF16)
    return dict(w_ml=w_ml, w_gc=w_gc, w_gr=w_gr, w_rest=w_rest, conv_w=conv_w, conv_b=conv_b, gbc=gbc, gbr=gbr,
                mlg=mlg, wcat=wcat, bias_gm=bias_gm, wb0=wb0,
                wb1=wb[1].astype(BF16), wb2=wb[2].astype(BF16), wb3=wb[3].astype(BF16))


def kernel(x, c, ctx, c_ctx, w_ada, b_ada, norm1_g, norm2_g, w_in, ml_conv_w, ml_conv_b, ml_gate_b, ml_norm_g,
           gm_ln_g, gm_ln_b, gm_w_s, gm_b_s, cv_dw_w, cv_dw_b, cv_ln_g, cv_ln_b, w_branch, w_out, peer_w_q,
           peer_keys, peer_u, peer_v, final_norm_g):
    b_sz, n_tok, d = x.shape
    n_ctx = ctx.shape[1]
    depth = w_ada.shape[0]
    ctx_row = b_sz

    cond = jnp.zeros((16, d), F32).at[:b_sz].set(c).at[ctx_row].set(c_ctx)
    mod_all = _ada_call(cond, w_ada, b_ada)
    x = _addpos_call(x, _grid_sincos(n_tok, d))
    xc = ctx
    cs = _channel_dft()
    dft_x = _dft_tables(n_tok)
    dft_c = _dft_tables(n_ctx)
    final_g = final_norm_g[None, :]
    lat_row = lambda i: i
    ctx_rowf = lambda i: ctx_row

    def mixers(xs, l, lw, mod, row_fn, dft, cn, m, need_out):
        p_ml, gc, gr = _inproj_ml_call(xs, mod, row_fn, norm1_g[l][None], lw["w_ml"], lw["w_gc"], lw["w_gr"])
        qk = _dwconv_call(p_ml, 0, 2 * ML_PAD_W, lw["conv_w"], lw["conv_b"], True)
        res = _mlstm_call(qk, p_ml, gc, gr, lw["gbc"], lw["gbr"], cn, m, need_out)
        if not need_out:
            return None, res[0], res[1]
        hf, hb, cn, m = res
        zgm, zglu, fa, fb, gate = _inproj_rest_call(xs, mod, row_fn, norm1_g[l][None], lw["w_rest"],
                                                    gm_ln_g[l][None], gm_ln_b[l][None], lw["wcat"],
                                                    lw["bias_gm"], cs)
        cv = _dwconv_call(zglu, 0, MIX_W, cv_dw_w[l], cv_dw_b[l][None], False)
        zft = _fourier_call(dft[0], dft[1], fa, fb)
        xo, h2 = _merge_call(xs, mod, row_fn, hf, hb, p_ml, zgm, cv, zft, gate, lw["mlg"], cv_ln_g[l][None],
                             cv_ln_b[l][None], norm2_g[l][None], lw["wb0"], lw["wb1"], lw["wb2"], lw["wb3"],
                             w_out[l].astype(BF16))
        return (xo, h2), cn, m

    def peer(xo, h2, l, mod, row_of_block, uv_pair, final, n_sc):
        bsz, t, _ = xo.shape
        n = bsz * t
        n_tc = n - n_sc
        h2f = h2.reshape(n, d)
        xf = xo.reshape(n, d)
        uv, uv2d = uv_pair
        e, g = _route_call(h2f, peer_w_q[l].astype(BF16), peer_keys[l].astype(BF16))
        out = _experts_call(e, g, h2f, xf, mod, row_of_block, final_g, uv, final, n_tc) if n_tc else xf
        if n_sc:
            p_sc = _sc_experts_call(e, g, h2f, uv2d, n_sc)
            out = _residual_call(xf, p_sc, out, mod, row_of_block, final_g, final)
        return out.reshape(bsz, t, d)

    groups = SC_GROUPS if b_sz % SC_GROUPS == 0 else 1
    gb = b_sz // groups
    use_sc = (gb * n_tok >= SC_TOKENS + PEER_TB and (gb * n_tok - SC_TOKENS) % PEER_TB == 0
              and (b_sz * n_ctx) % (SC_WORKERS * SC_TBATCH) == 0)
    blocks_per_batch = n_tok // PEER_TB
    xs = [x[i * gb:(i + 1) * gb] for i in range(groups)]

    for l in range(depth):
        last = l == depth - 1
        lw = _layer_weights(l, w_in, ml_conv_w, ml_conv_b, ml_gate_b, ml_norm_g, gm_w_s, gm_b_s, w_branch)
        mod = mod_all[l].reshape(16, 6, 1, d)
        half_tiles = (-1, ROW_TILES // 2, 1, LANE)
        uv = (jnp.concatenate([peer_u[l].reshape(half_tiles), peer_v[l].reshape(half_tiles)], axis=1),
              jnp.concatenate([peer_u[l], peer_v[l]], axis=1))
        cn0 = jnp.zeros((b_sz, 2 * ML_HEADS, HEAD_PAD, HEAD_PAD), F32)
        m0 = jnp.zeros((b_sz, 2 * ML_HEADS, 1, LANE), F32)
        res_c, cn, m = mixers(xc, l, lw, mod, ctx_rowf, dft_c, cn0, m0, not last)
        if not last:
            xc = peer(res_c[0], res_c[1], l, mod, lambda i: ctx_row, uv, False, b_sz * n_ctx if use_sc else 0)
        for gi in range(groups):
            b0 = gi * gb
            res_x, _, _ = mixers(xs[gi], l, lw, mod, lambda i, b0=b0: b0 + i, dft_x, cn[b0:b0 + gb], m[b0:b0 + gb], True)
            xs[gi] = peer(res_x[0], res_x[1], l, mod, lambda i, b0=b0: b0 + i // blocks_per_batch, uv, last,
                          SC_TOKENS if use_sc else 0)
    return jnp.concatenate(xs, axis=0)
```

```python
import functools
import math

import jax
import jax.numpy as jnp
from jax import lax
from jax.experimental import pallas as pl
from jax.experimental.pallas import tpu as pltpu
from jax.experimental.pallas import tpu_sc as plsc

F32 = jnp.float32
BF16 = jnp.bfloat16
HIGHEST = lax.Precision.HIGHEST

D_MODEL = 1024
GRID_W = 64
EPS = 1e-6
MIX_W = 256
ML_HEADS = 4
ML_DH = 64
ML_CHUNK = 128
GM_CHUNK = 128
GM_GROUPS = 4
GM_DG = 64
CV_K = 31
ML_CONV = 3
FT_GROUPS = 4
FT_DG = 64
PEER_HEADS = 8
PEER_TOPK = 16
N_KEYS = 128
PEER_SEL = PEER_HEADS * PEER_TOPK

LANE = 128
HEAD_PAD = LANE
ML_PAD_W = ML_HEADS * HEAD_PAD
CONV_HALO = 16
VMEM_LIMIT = 56 * 1024 * 1024


def _cparams(*sem):
    return pltpu.CompilerParams(dimension_semantics=sem, vmem_limit_bytes=VMEM_LIMIT)


def _sigmoid(x):
    return 1.0 / (1.0 + jnp.exp(-x))


def _silu(x):
    return x * _sigmoid(x)


def _gelu(x):
    return 0.5 * x * (1.0 + jnp.tanh(math.sqrt(2.0 / math.pi) * (x + 0.044715 * (x * x * x))))


def _log_sigmoid(x):
    return jnp.minimum(x, 0.0) - jnp.log(1.0 + jnp.exp(-jnp.abs(x)))


def _rms_mod(x, g, shift, scale):
    y = x * lax.rsqrt(jnp.mean(x * x, axis=-1, keepdims=True) + EPS) * g
    return y * (1.0 + scale) + shift


def _layernorm(x, g, b):
    mu = jnp.mean(x, axis=-1, keepdims=True)
    xc = x - mu
    return xc * lax.rsqrt(jnp.mean(xc * xc, axis=-1, keepdims=True) + EPS) * g + b


def _dot(a, b):
    return jnp.dot(a, b, preferred_element_type=F32)


def _dot_nt(a, b):
    return lax.dot_general(a, b, (((1,), (1,)), ((), ())), preferred_element_type=F32)


def _dot_exact(a, b):
    return jnp.dot(a, b, preferred_element_type=F32, precision=HIGHEST)


def _ada_kernel(s_ref, w_ref, b_ref, o_ref):
    s = _silu(s_ref[...])
    o_ref[0] = _dot_exact(s, w_ref[0]) + b_ref[0]


def _ada_call(cond, w_ada, b_ada):
    depth, d, n6 = w_ada.shape
    rows = cond.shape[0]
    tn = 1536
    return pl.pallas_call(
        _ada_kernel,
        out_shape=jax.ShapeDtypeStruct((depth, rows, n6), F32),
        grid=(depth, n6 // tn),
        in_specs=[pl.BlockSpec((rows, d), lambda l, j: (0, 0)),
                  pl.BlockSpec((1, d, tn), lambda l, j: (l, 0, j)),
                  pl.BlockSpec((1, 1, tn), lambda l, j: (l, 0, j))],
        out_specs=pl.BlockSpec((1, rows, tn), lambda l, j: (l, 0, j)),
        compiler_params=_cparams("arbitrary", "arbitrary"),
        name="ada_mod",
    )(cond, w_ada, b_ada.reshape(depth, 1, n6))


def _addpos_kernel(x_ref, p_ref, o_ref):
    o_ref[0] = x_ref[0] + p_ref[...]


def _addpos_call(x, pos):
    b, t, d = x.shape
    tm = min(t, 512)
    return pl.pallas_call(
        _addpos_kernel,
        out_shape=jax.ShapeDtypeStruct(x.shape, F32),
        grid=(t // tm, b),
        in_specs=[pl.BlockSpec((1, tm, d), lambda i, j: (j, i, 0)),
                  pl.BlockSpec((tm, d), lambda i, j: (i, 0))],
        out_specs=pl.BlockSpec((1, tm, d), lambda i, j: (j, i, 0)),
        compiler_params=_cparams("arbitrary", "arbitrary"),
        name="add_pos",
    )(x, pos)


def _mod_spec(row_fn, k):
    return pl.BlockSpec((1, 1, 1, D_MODEL), lambda *g: (row_fn(*g), k, 0, 0))


def _const_spec(shape):
    nd = len(shape)
    return pl.BlockSpec(shape, lambda *g: (0,) * nd)


def _inproj_ml_kernel(x_ref, sh_ref, sc_ref, g_ref, w_ref, wgc_ref, wgr_ref, p_ref, gc_ref, gr_ref):
    h = _rms_mod(x_ref[0], g_ref[...], sh_ref[0, 0], sc_ref[0, 0]).astype(BF16)
    p_ref[0] = _dot(h, w_ref[...])
    gc_ref[0] = _dot(h, wgc_ref[...])
    gr_ref[0] = _dot_nt(wgr_ref[...], h)


def _inproj_ml_call(x, mod, row_fn, norm_g, w_ml, w_gc, w_gr):
    b, t, d = x.shape
    tm = 256
    n_ml = w_ml.shape[1]
    return pl.pallas_call(
        _inproj_ml_kernel,
        out_shape=(jax.ShapeDtypeStruct((b, t, n_ml), F32),
                   jax.ShapeDtypeStruct((b, t, LANE), F32),
                   jax.ShapeDtypeStruct((b, 16, t), F32)),
        grid=(b, t // tm),
        in_specs=[pl.BlockSpec((1, tm, d), lambda i, j: (i, j, 0)),
                  _mod_spec(lambda i, j: row_fn(i), 0),
                  _mod_spec(lambda i, j: row_fn(i), 1),
                  _const_spec((1, d)),
                  _const_spec(w_ml.shape),
                  _const_spec(w_gc.shape),
                  _const_spec(w_gr.shape)],
        out_specs=(pl.BlockSpec((1, tm, n_ml), lambda i, j: (i, j, 0)),
                   pl.BlockSpec((1, tm, LANE), lambda i, j: (i, j, 0)),
                   pl.BlockSpec((1, 16, tm), lambda i, j: (i, 0, j))),
        compiler_params=_cparams("arbitrary", "arbitrary"),
        name="inproj_ml",
    )(x, mod, mod, norm_g, w_ml, w_gc, w_gr)


def _inproj_rest_kernel(x_ref, sh_ref, sc_ref, g_ref, w_ref, lng_ref, lnb_ref, wcat_ref, bgm_ref, cs_ref,
                        zgm_ref, zglu_ref, fa_ref, fb_ref, gate_ref):
    tm = x_ref.shape[1]
    h = _rms_mod(x_ref[0], g_ref[...], sh_ref[0, 0], sc_ref[0, 0]).astype(BF16)
    z = _gelu(_dot(h, w_ref[:, 0:2 * MIX_W]))
    u = z[:, :MIX_W]
    v = _layernorm(z[:, MIX_W:], lng_ref[...], lnb_ref[...])
    grp = lax.broadcasted_iota(jnp.int32, (GM_CHUNK, MIX_W), 1) // GM_DG
    for j in range(tm // GM_CHUNK):
        vc = v[j * GM_CHUNK:(j + 1) * GM_CHUNK]
        vstack = jnp.concatenate([jnp.where(grp == g, vc, 0.0) for g in range(GM_GROUPS)], axis=0).astype(BF16)
        s = _dot(wcat_ref[...], vstack) + bgm_ref[...]
        zgm_ref[0, j * GM_CHUNK:(j + 1) * GM_CHUNK, :] = u[j * GM_CHUNK:(j + 1) * GM_CHUNK] * s
    pc = _dot(h, w_ref[:, 2 * MIX_W:4 * MIX_W])
    zglu_ref[0] = pc[:, :MIX_W] * _sigmoid(pc[:, MIX_W:])
    pf = _dot(h, w_ref[:, 4 * MIX_W:5 * MIX_W]).astype(BF16)
    ab = _dot(pf, cs_ref[...])
    fa_ref[...] = ab[:, :MIX_W].astype(BF16)
    fb_ref[...] = ab[:, MIX_W:].astype(BF16)
    for i in range(4):
        lo = 5 * MIX_W + i * D_MODEL
        gate_ref[0, :, i * D_MODEL:(i + 1) * D_MODEL] = _sigmoid(_dot(h, w_ref[:, lo:lo + D_MODEL])).astype(BF16)


def _inproj_rest_call(x, mod, row_fn, norm_g, w_rest, gm_ln_g, gm_ln_b, wcat, bias_gm, cs):
    b, t, d = x.shape
    tm = 256
    return pl.pallas_call(
        _inproj_rest_kernel,
        out_shape=(jax.ShapeDtypeStruct((b, t, MIX_W), F32),
                   jax.ShapeDtypeStruct((b, t, MIX_W), F32),
                   jax.ShapeDtypeStruct((t, b * MIX_W), BF16),
                   jax.ShapeDtypeStruct((t, b * MIX_W), BF16),
                   jax.ShapeDtypeStruct((b, t, 4 * D_MODEL), BF16)),
        grid=(b, t // tm),
        in_specs=[pl.BlockSpec((1, tm, d), lambda i, j: (i, j, 0)),
                  _mod_spec(lambda i, j: row_fn(i), 0),
                  _mod_spec(lambda i, j: row_fn(i), 1),
                  _const_spec((1, d)),
                  _const_spec(w_rest.shape),
                  _const_spec((1, MIX_W)), _const_spec((1, MIX_W)),
                  _const_spec(wcat.shape), _const_spec(bias_gm.shape), _const_spec(cs.shape)],
        out_specs=(pl.BlockSpec((1, tm, MIX_W), lambda i, j: (i, j, 0)),
                   pl.BlockSpec((1, tm, MIX_W), lambda i, j: (i, j, 0)),
                   pl.BlockSpec((tm, MIX_W), lambda i, j: (j, i)),
                   pl.BlockSpec((tm, MIX_W), lambda i, j: (j, i)),
                   pl.BlockSpec((1, tm, 4 * D_MODEL), lambda i, j: (i, j, 0))),
        compiler_params=_cparams("arbitrary", "arbitrary"),
        name="inproj_rest",
    )(x, mod, mod, norm_g, w_rest, gm_ln_g, gm_ln_b, wcat, bias_gm, cs)


def _dwconv_kernel(x_ref, prev_ref, next_ref, w_ref, b_ref, o_ref, win_ref, *, taps, act):
    i = pl.program_id(2)
    last = pl.num_programs(2) - 1
    rc = x_ref.shape[1]
    win_ref[0:CONV_HALO, :] = jnp.where(i > 0, prev_ref[0], 0.0)
    win_ref[CONV_HALO:CONV_HALO + rc, :] = x_ref[0]
    win_ref[CONV_HALO + rc:, :] = jnp.where(i < last, next_ref[0], 0.0)
    sub = 128
    for r in range(rc // sub):
        acc = jnp.broadcast_to(b_ref[...], (sub, LANE))
        for j in range(taps):
            lo = CONV_HALO + r * sub + j - taps // 2
            acc = acc + w_ref[j:j + 1, :] * win_ref[lo:lo + sub, :]
        if act:
            acc = _silu(acc)
        o_ref[0, r * sub:(r + 1) * sub, :] = acc


def _dwconv_call(x, col0, ncol, w, bias, act):
    b, t, _ = x.shape
    taps = w.shape[0]
    rc = min(t, 512)
    c0 = col0 // LANE
    hb = rc // CONV_HALO
    nh = t // CONV_HALO
    return pl.pallas_call(
        functools.partial(_dwconv_kernel, taps=taps, act=act),
        out_shape=jax.ShapeDtypeStruct((b, t, ncol), F32),
        grid=(b, ncol // LANE, t // rc),
        in_specs=[pl.BlockSpec((1, rc, LANE), lambda bi, c, i: (bi, i, c0 + c)),
                  pl.BlockSpec((1, CONV_HALO, LANE), lambda bi, c, i: (bi, jnp.maximum(i * hb - 1, 0), c0 + c)),
                  pl.BlockSpec((1, CONV_HALO, LANE), lambda bi, c, i: (bi, jnp.minimum((i + 1) * hb, nh - 1), c0 + c)),
                  pl.BlockSpec((taps, LANE), lambda bi, c, i: (0, c)),
                  pl.BlockSpec((1, LANE), lambda bi, c, i: (0, c))],
        out_specs=pl.BlockSpec((1, rc, LANE), lambda bi, c, i: (bi, i, c)),
        scratch_shapes=[pltpu.VMEM((rc + 2 * CONV_HALO, LANE), F32)],
        compiler_params=_cparams("arbitrary", "arbitrary", "arbitrary"),
        name="dwconv_k%d" % taps,
    )(x, x, x, w, bias)


def _mlstm_kernel(*refs, need_out):
    (qkf_ref, qkb_ref, vf_ref, vb_ref, gcf_ref, gcb_ref, grf_ref, grb_ref, gbc_ref, gbr_ref,
     cn_in_ref, m_in_ref) = refs[:12]
    if need_out:
        hf_ref, hb_ref, cn_out_ref, m_out_ref, cn_s, m_s = refs[12:]
    else:
        cn_out_ref, m_out_ref, cn_s, m_s = refs[12:]
        hf_ref = hb_ref = None
    j = pl.program_id(1)
    last = pl.num_programs(1) - 1
    L = ML_CHUNK

    @pl.when(j == 0)
    def _():
        cn_s[...] = cn_in_ref[0]
        m_s[...] = m_in_ref[0]

    row = lax.broadcasted_iota(jnp.int32, (L, L), 0)
    col = lax.broadcasted_iota(jnp.int32, (L, L), 1)
    lane = lax.broadcasted_iota(jnp.int32, (1, HEAD_PAD), 1)
    one_at_dh = (lane == ML_DH).astype(F32)
    keep_dh = (lane < ML_DH).astype(F32)

    for direction in range(2):
        qk_ref, v_ref, gc_ref, gr_ref, h_ref = ((qkf_ref, vf_ref, gcf_ref, grf_ref, hf_ref) if direction == 0
                                                else (qkb_ref, vb_ref, gcb_ref, grb_ref, hb_ref))
        visible = (col <= row) if direction == 0 else (col >= row)
        tri_c = visible.astype(F32)
        tri_r = ((row <= col) if direction == 0 else (row >= col)).astype(F32)
        gcv = gc_ref[0] + gbc_ref[...]
        grv = gr_ref[0] + gbr_ref[...]
        lf_c = _log_sigmoid(gcv)
        lf_r = _log_sigmoid(grv)
        cum_c = _dot_exact(tri_c, lf_c)
        cum_r = _dot_exact(lf_r, tri_r)
        tot_c = jnp.sum(lf_c, axis=0, keepdims=True)
        for hd in range(ML_HEADS):
            g = direction * ML_HEADS + hd
            gi_col = g
            gf_col = 2 * ML_HEADS + g
            li_r = grv[gi_col:gi_col + 1, :]
            li_c = gcv[:, gi_col:gi_col + 1]
            cc = cum_c[:, gf_col:gf_col + 1]
            cr = cum_r[gf_col:gf_col + 1, :]
            tot = tot_c[:, gf_col:gf_col + 1]
            m_prev = m_s[g][:, 0:1]
            cn = cn_s[g]
            qh = qk_ref[0, :, hd * HEAD_PAD:(hd + 1) * HEAD_PAD]
            kh = qk_ref[0, :, ML_PAD_W + hd * HEAD_PAD:ML_PAD_W + (hd + 1) * HEAD_PAD] * (ML_DH ** -0.5)
            v1 = v_ref[0, :, hd * HEAD_PAD:(hd + 1) * HEAD_PAD] + one_at_dh
            w_r = tot - cr + li_r
            w_c = tot - cc + li_c
            m_new = jnp.maximum(tot + m_prev, jnp.max(w_r, axis=1, keepdims=True))
            ws_c = jnp.exp(w_c - m_new)
            decay = jnp.exp(tot + m_prev - m_new)
            if need_out:
                dlog = jnp.where(visible, cc - cr + li_r, -jnp.inf)
                inter = cc + m_prev
                m_t = jnp.maximum(inter, jnp.max(dlog, axis=1, keepdims=True))
                s = _dot_nt(qh.astype(BF16), kh.astype(BF16)) * jnp.exp(dlog - m_t)
                gi = jnp.exp(inter - m_t)
                nd = gi * _dot(qh.astype(BF16), cn.astype(BF16)) + _dot(s.astype(BF16), v1.astype(BF16))
                den = nd[:, ML_DH:ML_DH + 1]
                hh = nd / jnp.maximum(jnp.abs(den), jnp.exp(-m_t))
                h_ref[0, :, hd * HEAD_PAD:(hd + 1) * HEAD_PAD] = hh * keep_dh
            kw = (kh * ws_c).astype(BF16)
            upd = lax.dot_general(kw, v1.astype(BF16), (((0,), (0,)), ((), ())), preferred_element_type=F32)
            cn_s[g] = decay * cn + upd
            m_s[g] = jnp.broadcast_to(m_new, (1, LANE))

    @pl.when(j == last)
    def _():
        cn_out_ref[0] = cn_s[...]
        m_out_ref[0] = m_s[...]


def _mlstm_call(qk, p_ml, gc, gr, gbc, gbr, cn_in, m_in, need_out):
    b, t, _ = qk.shape
    L = ML_CHUNK
    nc = t // L
    g2 = 2 * ML_HEADS
    fwd = lambda i, j: (i, j, 0)
    bwd = lambda i, j: (i, nc - 1 - j, 0)
    in_specs = [pl.BlockSpec((1, L, 2 * ML_PAD_W), fwd),
                pl.BlockSpec((1, L, 2 * ML_PAD_W), bwd),
                pl.BlockSpec((1, L, ML_PAD_W), lambda i, j: (i, j, 2)),
                pl.BlockSpec((1, L, ML_PAD_W), lambda i, j: (i, nc - 1 - j, 2)),
                pl.BlockSpec((1, L, LANE), fwd),
                pl.BlockSpec((1, L, LANE), bwd),
                pl.BlockSpec((1, 16, L), lambda i, j: (i, 0, j)),
                pl.BlockSpec((1, 16, L), lambda i, j: (i, 0, nc - 1 - j)),
                _const_spec((1, LANE)),
                _const_spec((16, L)),
                pl.BlockSpec((1, g2, HEAD_PAD, HEAD_PAD), lambda i, j: (i, 0, 0, 0)),
                pl.BlockSpec((1, g2, 1, LANE), lambda i, j: (i, 0, 0, 0))]
    state_shapes = (jax.ShapeDtypeStruct((b, g2, HEAD_PAD, HEAD_PAD), F32),
                    jax.ShapeDtypeStruct((b, g2, 1, LANE), F32))
    state_specs = (pl.BlockSpec((1, g2, HEAD_PAD, HEAD_PAD), lambda i, j: (i, 0, 0, 0)),
                   pl.BlockSpec((1, g2, 1, LANE), lambda i, j: (i, 0, 0, 0)))
    if need_out:
        out_shape = (jax.ShapeDtypeStruct((b, t, ML_PAD_W), F32),
                     jax.ShapeDtypeStruct((b, t, ML_PAD_W), F32)) + state_shapes
        out_specs = (pl.BlockSpec((1, L, ML_PAD_W), fwd), pl.BlockSpec((1, L, ML_PAD_W), bwd)) + state_specs
    else:
        out_shape, out_specs = state_shapes, state_specs
    return pl.pallas_call(
        functools.partial(_mlstm_kernel, need_out=need_out),
        out_shape=out_shape,
        grid=(b, nc),
        in_specs=in_specs,
        out_specs=out_specs,
        scratch_shapes=[pltpu.VMEM((g2, HEAD_PAD, HEAD_PAD), F32), pltpu.VMEM((g2, 1, LANE), F32)],
        compiler_params=_cparams("arbitrary", "arbitrary"),
        name="mlstm_out" if need_out else "mlstm_state",
    )(qk, qk, p_ml, p_ml, gc, gc, gr, gr, gbc, gbr, cn_in, m_in)


def _fourier_kernel(ct_ref, st_ref, a_ref, b_ref, o_ref):
    @pl.when(pl.program_id(1) == 0)
    def _():
        o_ref[...] = jnp.zeros_like(o_ref)
    o_ref[...] += _dot(ct_ref[...], a_ref[...]) + _dot(st_ref[...], b_ref[...])


def _fourier_call(ct, mst, fa, fb):
    t, n = fa.shape
    tm = min(t, 512)
    return pl.pallas_call(
        _fourier_kernel,
        out_shape=jax.ShapeDtypeStruct((t, n), F32),
        grid=(t // tm, t // tm),
        in_specs=[pl.BlockSpec((tm, tm), lambda i, k: (i, k)),
                  pl.BlockSpec((tm, tm), lambda i, k: (i, k)),
                  pl.BlockSpec((tm, n), lambda i, k: (k, 0)),
                  pl.BlockSpec((tm, n), lambda i, k: (k, 0))],
        out_specs=pl.BlockSpec((tm, n), lambda i, k: (i, 0)),
        compiler_params=_cparams("arbitrary", "arbitrary"),
        name="fourier_pos",
    )(ct, mst, fa, fb)


def _merge_kernel(x_ref, m2_ref, m3_ref, m4_ref, hf_ref, hb_ref, o_ref, zgm_ref, cv_ref, zft_ref, gate_ref,
                  mlg_ref, cvg_ref, cvb_ref, n2g_ref, wb0_ref, wb1_ref, wb2_ref, wb3_ref, wout_ref,
                  xo_ref, h2_ref):
    hsum = hf_ref[0] + hb_ref[0]
    parts = []
    for hd in range(ML_HEADS):
        blk = hsum[:, hd * HEAD_PAD:(hd + 1) * HEAD_PAD]
        ms = jnp.sum(blk * blk, axis=-1, keepdims=True) * (1.0 / ML_DH)
        parts.append(blk * lax.rsqrt(ms + EPS))
    z_ml = jnp.concatenate(parts, axis=1) * mlg_ref[...] * _sigmoid(o_ref[0])
    z_cv = _silu(_layernorm(cv_ref[0], cvg_ref[...], cvb_ref[...]))

    def gate(i):
        return gate_ref[0, :, i * D_MODEL:(i + 1) * D_MODEL].astype(F32)

    y = gate(0) * _dot(z_ml.astype(BF16), wb0_ref[...])
    y = y + gate(1) * _dot(zgm_ref[0].astype(BF16), wb1_ref[...])
    y = y + gate(2) * _dot(z_cv.astype(BF16), wb2_ref[...])
    y = y + gate(3) * _dot(zft_ref[...].astype(BF16), wb3_ref[...])
    xo = x_ref[0] + m2_ref[0, 0] * _dot(y.astype(BF16), wout_ref[...])
    xo_ref[0] = xo
    h2_ref[0] = _rms_mod(xo, n2g_ref[...], m3_ref[0, 0], m4_ref[0, 0])


def _merge_call(x, mod, row_fn, hf, hb, p_ml, zgm, cv, zft, gate, mlg, cvg, cvb, n2g, wb0, wb1, wb2, wb3, wout):
    b, t, d = x.shape
    tm = 256
    tok = lambda w: pl.BlockSpec((1, tm, w), lambda i, j: (i, j, 0))
    rf = lambda i, j: row_fn(i)
    return pl.pallas_call(
        _merge_kernel,
        out_shape=(jax.ShapeDtypeStruct((b, t, d), F32), jax.ShapeDtypeStruct((b, t, d), F32)),
        grid=(b, t // tm),
        in_specs=[tok(d), _mod_spec(rf, 2), _mod_spec(rf, 3), _mod_spec(rf, 4),
                  tok(ML_PAD_W), tok(ML_PAD_W),
                  pl.BlockSpec((1, tm, ML_PAD_W), lambda i, j: (i, j, 3)),
                  tok(MIX_W), tok(MIX_W),
                  pl.BlockSpec((tm, MIX_W), lambda i, j: (j, i)),
                  tok(4 * d),
                  _const_spec((1, ML_PAD_W)), _const_spec((1, MIX_W)), _const_spec((1, MIX_W)), _const_spec((1, d)),
                  _const_spec(wb0.shape), _const_spec(wb1.shape), _const_spec(wb2.shape), _const_spec(wb3.shape),
                  _const_spec(wout.shape)],
        out_specs=(tok(d), tok(d)),
        compiler_params=_cparams("arbitrary", "arbitrary"),
        name="merge",
    )(x, mod, mod, mod, hf, hb, p_ml, zgm, cv, zft, gate, mlg, cvg, cvb, n2g, wb0, wb1, wb2, wb3, wout)


def _top16_rows(s, ids):
    big = jnp.float32(2 ** 30)
    out_rows = lax.broadcasted_iota(jnp.int32, (PEER_TOPK, LANE), 0)
    vals = jnp.zeros((PEER_TOPK, LANE), F32)
    sel = jnp.zeros((PEER_TOPK, LANE), F32)
    for i in range(PEER_TOPK):
        m = jnp.max(s, axis=0, keepdims=True)
        pick = jnp.min(jnp.where(s == m, ids, big), axis=0, keepdims=True)
        s = jnp.where(ids == pick, -jnp.inf, s)
        vals = jnp.where(out_rows == i, m, vals)
        sel = jnp.where(out_rows == i, pick, sel)
    return vals, sel


def _peer_stage2(v1, i1, v2, i2):
    big = jnp.float32(2 ** 30)
    sub_i = lax.broadcasted_iota(jnp.int32, (8, LANE), 0)
    sub = sub_i.astype(F32)
    tiles = []
    for a in range(8):
        va = v1[a:a + 1, :]
        ea = i1[a:a + 1, :] * float(N_KEYS)
        for half in range(2 if a == 0 else 1):
            ok = (a + 1) * (sub_i + 8 * half + 1) <= PEER_TOPK
            sc = jnp.where(ok, va + v2[8 * half:8 * half + 8, :], -jnp.inf)
            tiles.append((sc, sub + float(a * PEER_TOPK + 8 * half), ea + i2[8 * half:8 * half + 8, :]))
    tiles.append((v1[8:16, :] + v2[0:1, :], (sub + 8.0) * float(PEER_TOPK),
                  i1[8:16, :] * float(N_KEYS) + i2[0:1, :]))
    sc = jnp.concatenate([t[0] for t in tiles], axis=0)
    pos = jnp.concatenate([t[1] for t in tiles], axis=0)
    eid = jnp.concatenate([t[2] for t in tiles], axis=0)
    out_rows = lax.broadcasted_iota(jnp.int32, (PEER_TOPK, LANE), 0)
    vals = jnp.zeros((PEER_TOPK, LANE), F32)
    sel = jnp.zeros((PEER_TOPK, LANE), F32)
    for i in range(PEER_TOPK):
        m = jnp.max(sc, axis=0, keepdims=True)
        pick = jnp.min(jnp.where(sc == m, pos, big), axis=0, keepdims=True)
        hit = pos == pick
        e = jnp.max(jnp.where(hit, eid, -1.0), axis=0, keepdims=True)
        sc = jnp.where(hit, -jnp.inf, sc)
        vals = jnp.where(out_rows == i, m, vals)
        sel = jnp.where(out_rows == i, e, sel)
    return vals, sel


def _route_kernel(h_ref, wq_ref, keys_ref, e_ref, g_ref):
    tm = h_ref.shape[0]
    q = _dot(h_ref[...].astype(BF16), wq_ref[...])
    key_ids = lax.broadcasted_iota(jnp.int32, (N_KEYS, LANE), 0).astype(F32)
    for c in range(tm // LANE):
        ids, ws = [], []
        for hd in range(PEER_HEADS):
            tops = []
            for p in range(2):
                lo = (hd * 2 + p) * N_KEYS
                qs = q[c * LANE:(c + 1) * LANE, lo:lo + N_KEYS].astype(BF16)
                s = _dot_nt(keys_ref[p], qs)
                tops.append(_top16_rows(s, key_ids))
            sc, eid = _peer_stage2(tops[0][0], tops[0][1], tops[1][0], tops[1][1])
            w = jnp.exp(sc - sc[0:1, :])
            ids.append(eid)
            ws.append(w / jnp.sum(w, axis=0, keepdims=True))
        e_ref[c * LANE:(c + 1) * LANE, :] = jnp.concatenate(ids, axis=0).T.astype(jnp.int32)
        g_ref[c * LANE:(c + 1) * LANE, :] = jnp.concatenate(ws, axis=0).T


def _route_call(h2, wq, keys):
    n, d = h2.shape
    tm = 256
    return pl.pallas_call(
        _route_kernel,
        out_shape=(jax.ShapeDtypeStruct((n, PEER_SEL), jnp.int32), jax.ShapeDtypeStruct((n, PEER_SEL), F32)),
        grid=(n // tm,),
        in_specs=[pl.BlockSpec((tm, d), lambda i: (i, 0)), _const_spec(wq.shape), _const_spec(keys.shape)],
        out_specs=(pl.BlockSpec((tm, PEER_SEL), lambda i: (i, 0)), pl.BlockSpec((tm, PEER_SEL), lambda i: (i, 0))),
        compiler_params=_cparams("arbitrary"),
        name="peer_route",
    )(h2, wq, keys)


PEER_TB = 64
PEER_NBUF = 4
ROW_TILES = 2 * D_MODEL // LANE


def _experts_kernel(idx_ref, g_ref, h_ref, x_ref, m5_ref, fg_ref, uv_ref, fill_ref, o_ref, buf, sem, *, final):
    tb = h_ref.shape[0]
    nt = D_MODEL // LANE

    def gather(t, slot):
        for r in range(PEER_SEL):
            pltpu.make_async_copy(uv_ref.at[idx_ref[t, r]], buf.at[slot, :, pl.ds(r, 1), :],
                                  sem.at[slot]).start(priority=r % 2)

    def wait(slot):
        pltpu.make_async_copy(fill_ref, buf.at[slot], sem.at[slot]).wait()

    eye = (lax.broadcasted_iota(jnp.int32, (PEER_SEL, PEER_SEL), 0)
           == lax.broadcasted_iota(jnp.int32, (PEER_SEL, PEER_SEL), 1))

    def u_of(w):
        return lax.bitcast_convert_type(jnp.left_shift(w, jnp.uint32(16)), F32)

    def v_of(w):
        return lax.bitcast_convert_type(jnp.bitwise_and(w, jnp.uint32(0xFFFF0000)), F32)

    def compute(t, slot):
        hrow = h_ref[pl.ds(t, 1), :]
        acc = u_of(buf[slot, 0]) * hrow[:, 0:LANE]
        for c in range(1, nt):
            acc = acc + u_of(buf[slot, c]) * hrow[:, c * LANE:(c + 1) * LANE]
        s = jnp.sum(acc, axis=1, keepdims=True)
        gcol = jnp.sum(jnp.where(eye, g_ref[pl.ds(t, 1), :], 0.0), axis=1, keepdims=True)
        a = _gelu(s) * gcol
        o = jnp.concatenate([jnp.sum(v_of(buf[slot, c]) * a, axis=0, keepdims=True) for c in range(nt)], axis=1)
        xo = x_ref[pl.ds(t, 1), :] + m5_ref[0, 0] * o
        if final:
            xo = xo * lax.rsqrt(jnp.mean(xo * xo, axis=-1, keepdims=True) + EPS) * fg_ref[...]
        o_ref[pl.ds(t, 1), :] = xo

    for slot in range(PEER_NBUF):
        gather(slot, slot)

    def body(i, carry):
        for slot in range(PEER_NBUF):
            t = i * PEER_NBUF + slot
            wait(slot)
            compute(t, slot)

            @pl.when(t + PEER_NBUF < tb)
            def _():
                gather(t + PEER_NBUF, slot)
        return carry

    lax.fori_loop(0, tb // PEER_NBUF, body, 0)


def _experts_call(idx, g, h2, x, mod, row_of_block, final_g, uv, final, n_tc):
    n, d = h2.shape
    tb = PEER_TB
    tok = lambda w: pl.BlockSpec((tb, w), lambda i: (i, 0))
    fill = jnp.zeros((d // LANE, PEER_SEL, LANE), jnp.uint32)
    return pl.pallas_call(
        functools.partial(_experts_kernel, final=final),
        out_shape=jax.ShapeDtypeStruct((n, d), F32),
        grid=(n_tc // tb,),
        in_specs=[pl.BlockSpec((tb, PEER_SEL), lambda i: (i, 0), memory_space=pltpu.SMEM),
                  tok(PEER_SEL), tok(d), tok(d),
                  _mod_spec(lambda i: row_of_block(i), 5),
                  _const_spec((1, d)),
                  pl.BlockSpec(memory_space=pl.ANY),
                  pl.BlockSpec(memory_space=pl.ANY)],
        out_specs=tok(d),
        scratch_shapes=[pltpu.VMEM((PEER_NBUF, d // LANE, PEER_SEL, LANE), jnp.uint32),
                        pltpu.SemaphoreType.DMA((PEER_NBUF,))],
        compiler_params=_cparams("arbitrary"),
        name="peer_experts_final" if final else "peer_experts",
    )(idx, g, h2, x, mod, final_g, uv, fill)


SC_CORES = 2
SC_SUBCORES = 16
SC_LANES = 16
SC_WORKERS = SC_CORES * SC_SUBCORES
SC_ROWS = 16
SC_TBATCH = 8
SC_GROUPS = 4
SC_TOKENS = 4864
SC_TAIL_SHIFT = SC_WORKERS * SC_TBATCH


def _sc_gelu(x):
    y = math.sqrt(2.0 / math.pi) * (x + 0.044715 * (x * x * x))
    tanh_y = 1.0 - 2.0 / (jnp.exp(2.0 * y) + 1.0)
    return 0.5 * x * (1.0 + tanh_y)


def _sc_experts_call(idx, g, h2, uv2d, ns):
    n, d = h2.shape
    tok0 = n - ns
    per = ns // SC_WORKERS
    nv = d // SC_LANES
    mesh = plsc.VectorSubcoreMesh(core_axis_name="c", subcore_axis_name="s")

    nch = PEER_SEL // SC_ROWS
    nq = SC_TBATCH * nch
    assert SC_ROWS == SC_LANES and nch & (nch - 1) == 0 and per % SC_TBATCH == 0
    shift = nch.bit_length() - 1

    def body(idx_hbm, g_hbm, h_hbm, uv_hbm, out_hbm, idx_v, g_v, h_v, rows_a, rows_b, out_v, sem_a, sem_b):
        wid = lax.axis_index("s") * SC_CORES + lax.axis_index("c")
        base = wid * per
        lane = lax.iota(jnp.int32, SC_LANES)
        zero = jnp.zeros((SC_LANES,), F32)

        def gather(q, rows, sem):
            tl = lax.shift_right_logical(q, shift)
            ch = jnp.bitwise_and(q, nch - 1)
            ids = idx_v[tl, pl.ds(ch * SC_ROWS, SC_ROWS)]
            return pltpu.make_async_copy(uv_hbm.at[ids], rows, sem)

        def compute(q, rows):
            tl = lax.shift_right_logical(q, shift)
            ch = jnp.bitwise_and(q, nch - 1)

            def ubody(j, accs):
                hj = h_v[tl, pl.ds(j * SC_LANES, SC_LANES)]
                return tuple(accs[i] + rows[i, pl.ds(j * SC_LANES, SC_LANES)] * hj for i in range(SC_LANES))
            accs = lax.fori_loop(0, nv, ubody, tuple(zero for _ in range(SC_LANES)))
            s = zero
            for i in range(SC_LANES):
                s = jnp.where(lane == i, jnp.sum(accs[i]), s)
            a = _sc_gelu(s) * g_v[tl, pl.ds(ch * SC_ROWS, SC_LANES)]
            ab = [jnp.sum(jnp.where(lane == i, a, 0.0)) for i in range(SC_LANES)]

            def vbody(cb, c):
                col = cb * (SC_LANES * SC_LANES)
                acc = [out_v[tl, pl.ds(col + k * SC_LANES, SC_LANES)] for k in range(SC_LANES)]
                for i in range(SC_LANES):
                    for k in range(SC_LANES):
                        acc[k] = acc[k] + rows[i, pl.ds(d + col + k * SC_LANES, SC_LANES)] * ab[i]
                for k in range(SC_LANES):
                    out_v[tl, pl.ds(col + k * SC_LANES, SC_LANES)] = acc[k]
                return c
            lax.fori_loop(0, d // (SC_LANES * SC_LANES), vbody, 0)

        def batch(bi, carry):
            t0 = base + bi * SC_TBATCH
            pltpu.sync_copy(idx_hbm.at[pl.ds(tok0 + t0, SC_TBATCH)], idx_v)
            pltpu.sync_copy(g_hbm.at[pl.ds(tok0 + t0, SC_TBATCH)], g_v)
            pltpu.sync_copy(h_hbm.at[pl.ds(tok0 + t0, SC_TBATCH)], h_v)

            def clear(j, c):
                for tl in range(SC_TBATCH):
                    out_v[tl, pl.ds(j * SC_LANES, SC_LANES)] = zero
                return c
            lax.fori_loop(0, nv, clear, 0)

            gather(0, rows_a, sem_a).start()

            def pair(p, c):
                q = 2 * p
                gather(q + 1, rows_b, sem_b).start()
                gather(q, rows_a, sem_a).wait()
                compute(q, rows_a)

                @pl.when(q + 2 < nq)
                def _():
                    gather(q + 2, rows_a, sem_a).start()
                gather(q + 1, rows_b, sem_b).wait()
                compute(q + 1, rows_b)
                return c
            lax.fori_loop(0, nq // 2, pair, 0)
            pltpu.sync_copy(out_v, out_hbm.at[pl.ds(t0, SC_TBATCH)])
            return carry

        lax.fori_loop(0, per // SC_TBATCH, batch, 0)

    return pl.kernel(
        body,
        out_type=jax.ShapeDtypeStruct((ns, d), F32),
        mesh=mesh,
        scratch_types=[pltpu.VMEM((SC_TBATCH, PEER_SEL), jnp.int32), pltpu.VMEM((SC_TBATCH, PEER_SEL), F32),
                       pltpu.VMEM((SC_TBATCH, d), F32),
                       pltpu.VMEM((SC_ROWS, 2 * d), F32), pltpu.VMEM((SC_ROWS, 2 * d), F32),
                       pltpu.VMEM((SC_TBATCH, d), F32),
                       pltpu.SemaphoreType.DMA, pltpu.SemaphoreType.DMA],
        compiler_params=pltpu.CompilerParams(needs_layout_passes=False),
        name="peer_experts_sc",
    )(idx, g, h2, uv2d)


def _residual_kernel(x_ref, p_ref, m5_ref, fg_ref, prev_ref, o_ref, *, final):
    del prev_ref
    xo = x_ref[...] + m5_ref[0, 0] * p_ref[...]
    if final:
        xo = xo * lax.rsqrt(jnp.mean(xo * xo, axis=-1, keepdims=True) + EPS) * fg_ref[...]
    o_ref[...] = xo


def _residual_call(x, p, prev, mod, row_of_block, final_g, final):
    n, d = x.shape
    ns = p.shape[0]
    tb = PEER_TB
    first = (n - ns) // tb
    tail = pl.BlockSpec((tb, d), lambda i: (i + first, 0))
    return pl.pallas_call(
        functools.partial(_residual_kernel, final=final),
        out_shape=jax.ShapeDtypeStruct((n, d), F32),
        grid=(ns // tb,),
        in_specs=[tail, pl.BlockSpec((tb, d), lambda i: (i, 0)),
                  _mod_spec(lambda i: row_of_block(i + first), 5), _const_spec((1, d)),
                  pl.BlockSpec(memory_space=pl.ANY)],
        out_specs=tail,
        input_output_aliases={4: 0},
        compiler_params=_cparams("arbitrary"),
        name="peer_residual",
    )(x, p, mod, final_g, prev)


def _grid_sincos(n_tok, d):
    rows = n_tok // GRID_W
    n_freq = d // 4
    freq = 1.0 / (10000.0 ** (jnp.arange(n_freq, dtype=F32) / n_freq))
    r = jnp.repeat(jnp.arange(rows, dtype=F32), GRID_W)
    cc = jnp.tile(jnp.arange(GRID_W, dtype=F32), rows)
    ar = r[:, None] * freq[None, :]
    ac = cc[:, None] * freq[None, :]
    return jnp.concatenate([jnp.sin(ar), jnp.cos(ar), jnp.sin(ac), jnp.cos(ac)], axis=-1)


def _dft_tables(t_len):
    i = jnp.arange(t_len, dtype=jnp.int32)
    ph = (i[:, None] * i[None, :]) % t_len
    ang = ph.astype(F32) * (2.0 * math.pi / t_len)
    scale = 1.0 / math.sqrt(t_len * FT_DG)
    return (jnp.cos(ang) * scale).astype(BF16), (-jnp.sin(ang) * scale).astype(BF16)


def _channel_dft():
    i = jnp.arange(MIX_W, dtype=jnp.int32)
    same = (i[:, None] // FT_DG) == (i[None, :] // FT_DG)
    ph = ((i[:, None] % FT_DG) * (i[None, :] % FT_DG)) % FT_DG
    ang = ph.astype(F32) * (2.0 * math.pi / FT_DG)
    c = jnp.where(same, jnp.cos(ang), 0.0)
    s = jnp.where(same, jnp.sin(ang), 0.0)
    return jnp.concatenate([c, s], axis=1).astype(BF16)


def _pad_heads(a):
    lead = a.shape[:-1]
    a = a.reshape(lead + (ML_HEADS, ML_DH))
    a = jnp.pad(a, [(0, 0)] * len(lead) + [(0, 0), (0, HEAD_PAD - ML_DH)])
    return a.reshape(lead + (ML_PAD_W,))


def _layer_weights(l, w_in, ml_conv_w, ml_conv_b, ml_gate_b, ml_norm_g, gm_w_s, gm_b_s, w_branch):
    w = w_in[l]
    q, k, v = w[:, 0:256], w[:, 256:512], w[:, 512:768]
    gates, o = w[:, 768:784], w[:, 784:1040]
    w_ml = jnp.concatenate([_pad_heads(q), _pad_heads(k), _pad_heads(v), _pad_heads(o)], axis=1).astype(BF16)
    w_gc = jnp.pad(gates, ((0, 0), (0, LANE - 16))).astype(BF16)
    w_gr = gates.T.astype(BF16)
    w_rest = w[:, 1040:].astype(BF16)
    cw = ml_conv_w[l]
    conv_w = jnp.concatenate([_pad_heads(cw[:, :256]), _pad_heads(cw[:, 256:])], axis=1)
    cb = ml_conv_b[l]
    conv_b = jnp.concatenate([_pad_heads(cb[:256]), _pad_heads(cb[256:])])[None, :]
    gb = ml_gate_b[l]
    gbc = jnp.pad(gb, (0, LANE - 16))[None, :]
    gbr = jnp.broadcast_to(gb[:, None], (16, ML_CHUNK))
    mlg = _pad_heads(ml_norm_g[l].reshape(-1))[None, :]
    wcat = jnp.transpose(gm_w_s[l], (1, 0, 2)).reshape(GM_CHUNK, GM_GROUPS * GM_CHUNK).astype(BF16)
    bias_gm = jnp.repeat(gm_b_s[l].T, GM_DG, axis=1)
    wb = w_branch[l]
    wb0 = jnp.pad(wb[0].reshape(ML_HEADS, ML_DH, D_MODEL), ((0, 0), (0, HEAD_PAD - ML_DH), (0, 0)))
    wb0 = wb0.reshape(ML_PAD_W, D_MODEL).astype(BF16)
    return dict(w_ml=w_ml, w_gc=w_gc, w_gr=w_gr, w_rest=w_rest, conv_w=conv_w, conv_b=conv_b, gbc=gbc, gbr=gbr,
                mlg=mlg, wcat=wcat, bias_gm=bias_gm, wb0=wb0,
                wb1=wb[1].astype(BF16), wb2=wb[2].astype(BF16), wb3=wb[3].astype(BF16))


def kernel(x, c, ctx, c_ctx, w_ada, b_ada, norm1_g, norm2_g, w_in, ml_conv_w, ml_conv_b, ml_gate_b, ml_norm_g,
           gm_ln_g, gm_ln_b, gm_w_s, gm_b_s, cv_dw_w, cv_dw_b, cv_ln_g, cv_ln_b, w_branch, w_out, peer_w_q,
           peer_keys, peer_u, peer_v, final_norm_g):
    b_sz, n_tok, d = x.shape
    n_ctx = ctx.shape[1]
    depth = w_ada.shape[0]
    ctx_row = b_sz

    cond = jnp.zeros((16, d), F32).at[:b_sz].set(c).at[ctx_row].set(c_ctx)
    mod_all = _ada_call(cond, w_ada, b_ada)
    x = _addpos_call(x, _grid_sincos(n_tok, d))
    xc = ctx
    cs = _channel_dft()
    dft_x = _dft_tables(n_tok)
    dft_c = _dft_tables(n_ctx)
    final_g = final_norm_g[None, :]
    lat_row = lambda i: i
    ctx_rowf = lambda i: ctx_row

    def mixers(xs, l, lw, mod, row_fn, dft, cn, m, need_out):
        p_ml, gc, gr = _inproj_ml_call(xs, mod, row_fn, norm1_g[l][None], lw["w_ml"], lw["w_gc"], lw["w_gr"])
        qk = _dwconv_call(p_ml, 0, 2 * ML_PAD_W, lw["conv_w"], lw["conv_b"], True)
        res = _mlstm_call(qk, p_ml, gc, gr, lw["gbc"], lw["gbr"], cn, m, need_out)
        if not need_out:
            return None, res[0], res[1]
        hf, hb, cn, m = res
        zgm, zglu, fa, fb, gate = _inproj_rest_call(xs, mod, row_fn, norm1_g[l][None], lw["w_rest"],
                                                    gm_ln_g[l][None], gm_ln_b[l][None], lw["wcat"],
                                                    lw["bias_gm"], cs)
        cv = _dwconv_call(zglu, 0, MIX_W, cv_dw_w[l], cv_dw_b[l][None], False)
        zft = _fourier_call(dft[0], dft[1], fa, fb)
        xo, h2 = _merge_call(xs, mod, row_fn, hf, hb, p_ml, zgm, cv, zft, gate, lw["mlg"], cv_ln_g[l][None],
                             cv_ln_b[l][None], norm2_g[l][None], lw["wb0"], lw["wb1"], lw["wb2"], lw["wb3"],
                             w_out[l].astype(BF16))
        return (xo, h2), cn, m

    def peer(xo, h2, l, mod, row_of_block, uv_pair, final, n_sc):
        bsz, t, _ = xo.shape
        n = bsz * t
        n_tc = n - n_sc
        h2f = h2.reshape(n, d)
        xf = xo.reshape(n, d)
        uv, uv2d = uv_pair
        e, g = _route_call(h2f, peer_w_q[l].astype(BF16), peer_keys[l].astype(BF16))
        out = _experts_call(e, g, h2f, xf, mod, row_of_block, final_g, uv, final, n_tc) if n_tc else xf
        if n_sc:
            p_sc = _sc_experts_call(e, g, h2f, uv2d, n_sc)
            out = _residual_call(xf, p_sc, out, mod, row_of_block, final_g, final)
        return out.reshape(bsz, t, d)

    groups = SC_GROUPS if b_sz % SC_GROUPS == 0 else 1
    gb = b_sz // groups
    use_sc = (gb * n_tok >= SC_TOKENS + PEER_TB and (gb * n_tok - SC_TOKENS) % PEER_TB == 0
              and (b_sz * n_ctx) % (SC_WORKERS * SC_TBATCH) == 0)
    blocks_per_batch = n_tok // PEER_TB
    xs = [x[i * gb:(i + 1) * gb] for i in range(groups)]

    for l in range(depth):
        last = l == depth - 1
        lw = _layer_weights(l, w_in, ml_conv_w, ml_conv_b, ml_gate_b, ml_norm_g, gm_w_s, gm_b_s, w_branch)
        mod = mod_all[l].reshape(16, 6, 1, d)
        def bf16_bits(a):
            return lax.bitcast_convert_type(a.astype(BF16), jnp.uint16).astype(jnp.uint32)
        packed = jnp.bitwise_or(bf16_bits(peer_u[l]), jnp.left_shift(bf16_bits(peer_v[l]), jnp.uint32(16)))
        uv = (packed.reshape(-1, d // LANE, 1, LANE), jnp.concatenate([peer_u[l], peer_v[l]], axis=1))
        cn0 = jnp.zeros((b_sz, 2 * ML_HEADS, HEAD_PAD, HEAD_PAD), F32)
        m0 = jnp.zeros((b_sz, 2 * ML_HEADS, 1, LANE), F32)
        res_c, cn, m = mixers(xc, l, lw, mod, ctx_rowf, dft_c, cn0, m0, not last)
        if not last:
            xc = peer(res_c[0], res_c[1], l, mod, lambda i: ctx_row, uv, False, b_sz * n_ctx if use_sc else 0)
        for gi in range(groups):
            b0 = gi * gb
            res_x, _, _ = mixers(xs[gi], l, lw, mod, lambda i, b0=b0: b0 + i, dft_x, cn[b0:b0 + gb], m[b0:b0 + gb], True)
            n_sc = SC_TOKENS - (SC_TAIL_SHIFT if last and gi >= groups - 2 else 0)
            xs[gi] = peer(res_x[0], res_x[1], l, mod, lambda i, b0=b0: b0 + i // blocks_per_batch, uv, last,
                          n_sc if use_sc else 0)
    return jnp.concatenate(xs, axis=0)
```
